```python
import jax, jax.numpy as jnp
from jax import lax
import numpy as np

D_MODEL = 1024
BATCH = 4
SEQ = 4096
DEPTH = 4
DEC_BATCH = 128
DEC_SEQ = 8
PAST_LEN = 2048
PAGE_SIZE = 128

N_A = DEPTH // 2
N_B = DEPTH - N_A
POOL_WINDOWS = (2, 4, 8, 16)
N_POOL_GROUPS = len(POOL_WINDOWS)
POOL_GROUP = D_MODEL // N_POOL_GROUPS
POOL_HIST = max(POOL_WINDOWS) - 1
HEAD_DIM = 64
N_HEADS = D_MODEL // HEAD_DIM
DIL_PATTERNS = ((128, 1), (512, 4), (2048, 16))
N_DIL = len(DIL_PATTERNS)
W_MAX = max(w for w, _ in DIL_PATTERNS)
D_MAX = max(d for _, d in DIL_PATTERNS)
Q_BLOCK = 128
N_EXPERTS = 16
N_EXPERT_GROUPS = 4
EXPERTS_PER_GROUP = N_EXPERTS // N_EXPERT_GROUPS
TOP_K = 2
D_EXPERT = 512
D_PLE = 256
ALPHA = (2 * DEPTH) ** 0.25
BETA = (8 * DEPTH) ** -0.25
LN_EPS = 1e-5

kernel_name = 'yoco_pool_dilated_moe_step'


def alibi_slopes():
    n = N_DIL * N_HEADS
    return (2.0 ** (-8.0 * np.arange(1, n + 1) / n)).astype(np.float32).reshape(N_DIL, N_HEADS)


def layer_norm(x, g, b):
    xf = x.astype(jnp.float32)
    mu = jnp.mean(xf, -1, keepdims=True)
    var = jnp.mean(jnp.square(xf - mu), -1, keepdims=True)
    return ((xf - mu) * lax.rsqrt(var + LN_EPS) * g.astype(jnp.float32) + b.astype(jnp.float32)).astype(x.dtype)


def pool_mixer(x, prefix, prefix_valid, w_pool, scale):
    B, T, D = x.shape
    ext = jnp.concatenate([prefix.astype(x.dtype), x], axis=1)
    valid = jnp.concatenate([jnp.full((POOL_HIST,), prefix_valid, jnp.float32), jnp.ones((T,), jnp.float32)])
    cs = jnp.concatenate([jnp.zeros((B, 1, D), jnp.float32),
                          jnp.cumsum(ext.astype(jnp.float32) * valid[None, :, None], axis=1)], axis=1)
    cn = jnp.concatenate([jnp.zeros((1,), jnp.float32), jnp.cumsum(valid)])
    end = POOL_HIST + 1
    means = []
    for g, w in enumerate(POOL_WINDOWS):
        ch = slice(g * POOL_GROUP, (g + 1) * POOL_GROUP)
        s = cs[:, end:end + T, ch] - cs[:, end - w:end - w + T, ch]
        n = cn[end:end + T] - cn[end - w:end - w + T]
        means.append(s / n[None, :, None])
    diff = jnp.stack(means, axis=2) - x.astype(jnp.float32).reshape(B, T, N_POOL_GROUPS, POOL_GROUP)
    y = jnp.einsum('btgc,gcd->btgd', diff, w_pool.astype(jnp.float32))
    y = y * scale.astype(jnp.float32).reshape(N_POOL_GROUPS, POOL_GROUP)
    return y.reshape(B, T, D).astype(x.dtype), ext[:, -POOL_HIST:]


def dilated_blocks(q, k_ext, v_ext, q_pos0, qb):
    B, Tp = q.shape[:2]
    nb = Tp // qb
    slopes = jnp.asarray(alibi_slopes())
    scale = HEAD_DIM ** -0.5

    def block(bi):
        s0 = bi * qb
        qblk = lax.dynamic_slice_in_dim(q, s0, qb, axis=1).astype(jnp.float32)
        outs, lses = [], []
        for g, (w, d) in enumerate(DIL_PATTERNS):
            L = w + qb
            kb = lax.dynamic_slice_in_dim(k_ext, W_MAX + s0 - w, L, axis=1).astype(jnp.float32)
            vb = lax.dynamic_slice_in_dim(v_ext, W_MAX + s0 - w, L, axis=1).astype(jnp.float32)
            I, J = qb // d, L // d
            qg = qblk[:, :, g].reshape(B, I, d, N_HEADS, HEAD_DIM)
            kg = kb.reshape(B, J, d, N_HEADS, HEAD_DIM)
            vg = vb.reshape(B, J, d, N_HEADS, HEAD_DIM)
            s = jnp.einsum('birhc,bjrhc->brhij', qg, kg) * scale
            ii = jnp.arange(I)[:, None]
            jj = jnp.arange(J)[None, :]
            dist = w + (ii - jj) * d
            kpos = (q_pos0 + s0 - w) + jj[None] * d + jnp.arange(d)[:, None, None]
            ok = (dist >= 0) & (dist <= w) & (kpos >= 0)
            bias = -slopes[g][:, None, None] * dist.astype(jnp.float32)[None]
            s = jnp.where(ok[None, :, None], s + bias[None, None], -jnp.inf)
            lse = jax.nn.logsumexp(s, axis=-1)
            p = jnp.exp(s - lse[..., None])
            o = jnp.einsum('brhij,bjrhc->birhc', p, vg).reshape(B, qb, N_HEADS, HEAD_DIM)
            outs.append(o)
            lses.append(jnp.transpose(lse, (0, 3, 1, 2)).reshape(B, qb, N_HEADS))
        wgt = jax.nn.softmax(jnp.stack(lses, 0), axis=0)
        return jnp.sum(wgt[..., None] * jnp.stack(outs, 0), axis=0)

    o = lax.map(block, jnp.arange(nb))
    return jnp.transpose(o, (1, 0, 2, 3, 4)).reshape(B, Tp, N_HEADS * HEAD_DIM)


def dilated_attention(q, k_new, v_new, k_prefix, v_prefix, q_pos0):
    B, T = q.shape[:2]
    qb = min(Q_BLOCK, -(-T // D_MAX) * D_MAX)
    Tp = -(-T // qb) * qb
    pad = Tp - T
    if k_prefix is None:
        k_all, v_all, front = k_new, v_new, W_MAX
    else:
        k_all = jnp.concatenate([k_prefix.astype(k_new.dtype), k_new], axis=1)
        v_all = jnp.concatenate([v_prefix.astype(v_new.dtype), v_new], axis=1)
        front = W_MAX - k_prefix.shape[1]
    padw = ((0, 0), (front, pad), (0, 0), (0, 0))
    k_ext = jnp.pad(k_all, padw)
    v_ext = jnp.pad(v_all, padw)
    qp = jnp.pad(q, ((0, 0), (0, pad), (0, 0), (0, 0), (0, 0)))
    return dilated_blocks(qp, k_ext, v_ext, q_pos0, qb)[:, :T]


def grouped_moe(x, w_router, b_router, w_gate, w_up, w_down):
    shp = x.shape
    xf = x.reshape(-1, shp[-1])
    n = xf.shape[0]
    s = jax.nn.sigmoid(jnp.dot(xf, w_router, preferred_element_type=jnp.float32))
    sel = (s + b_router.astype(jnp.float32)).reshape(n, N_EXPERT_GROUPS, EXPERTS_PER_GROUP)
    group_score = jnp.sum(lax.top_k(sel, TOP_K)[0], axis=-1)
    g_idx = jnp.argmax(group_score, axis=-1)
    in_group = jnp.take_along_axis(sel, g_idx[:, None, None], axis=1)[:, 0]
    loc = lax.top_k(in_group, TOP_K)[1]
    e_idx = g_idx[:, None] * EXPERTS_PER_GROUP + loc
    w_sel = jnp.take_along_axis(s, e_idx, axis=1)
    w_sel = w_sel / jnp.sum(w_sel, -1, keepdims=True)
    gates = jnp.sum(jax.nn.one_hot(e_idx, N_EXPERTS, dtype=jnp.float32) * w_sel[..., None], axis=1)
    y = jnp.zeros(xf.shape, jnp.float32)
    for e in range(N_EXPERTS):
        h = jax.nn.silu(xf @ w_gate[e]) * (xf @ w_up[e])
        y = y + gates[:, e:e + 1] * (h @ w_down[e]).astype(jnp.float32)
    return y.astype(x.dtype).reshape(shp)


def trunk(x, p, pool_prefix, prefix_valid, k_prefix, v_prefix, q_pos0,
          w_pool, pool_scale, w_kv, w_q, w_o, ln_g, ln_b, w_router, b_router,
          w_exp_gate, w_exp_up, w_exp_down, w_ple_gate, w_ple_proj):
    B, T, D = x.shape
    pool_states = []
    k_new = v_new = None
    for i in range(DEPTH):
        if i < N_A:
            mix, st = pool_mixer(x, pool_prefix[i], prefix_valid, w_pool[i], pool_scale[i])
            pool_states.append(st)
        else:
            j = i - N_A
            q = (x @ w_q[j]).reshape(B, T, N_DIL, N_HEADS, HEAD_DIM)
            o = dilated_attention(q, k_new, v_new, k_prefix, v_prefix, q_pos0)
            mix = o.astype(x.dtype) @ w_o[j]
        x = layer_norm(ALPHA * x + mix, ln_g[i, 0], ln_b[i, 0])
        x = layer_norm(ALPHA * x + grouped_moe(x, w_router, b_router, w_exp_gate[i], w_exp_up[i], w_exp_down[i]),
                       ln_g[i, 1], ln_b[i, 1])
        ple = jax.nn.sigmoid(x @ w_ple_gate[i]) * (p[i].astype(x.dtype) @ w_ple_proj[i])
        x = layer_norm(ALPHA * x + ple, ln_g[i, 2], ln_b[i, 2])
        if i == N_A - 1:
            kv = (x @ w_kv).reshape(B, T, 2, N_HEADS, HEAD_DIM)
            k_new, v_new = kv[:, :, 0], kv[:, :, 1]
    return x, jnp.stack(pool_states, 0), k_new, v_new


def setup_inputs(seed: int = 0) -> dict:
    key = jax.random.key(seed)
    ks = jax.random.split(key, 24)
    f32 = jnp.float32

    def nrm(k, shape, fan_in, gain=1.0):
        return jax.random.normal(k, shape, f32) * (gain * fan_in ** -0.5)

    w_buf = min(W_MAX, PAST_LEN)
    return {
        'x_prompt': jax.random.normal(ks[0], (BATCH, SEQ, D_MODEL), f32),
        'x_sample': jax.random.normal(ks[1], (DEC_BATCH, DEC_SEQ, D_MODEL), f32),
        'state_pool': jax.random.normal(ks[2], (N_A, DEC_BATCH, POOL_HIST, D_MODEL), f32),
        'cache_k': jax.random.normal(ks[3], (DEC_BATCH, w_buf, N_HEADS, HEAD_DIM), f32),
        'cache_v': jax.random.normal(ks[4], (DEC_BATCH, w_buf, N_HEADS, HEAD_DIM), f32),
        'p_prompt': jax.random.normal(ks[5], (DEPTH, BATCH, SEQ, D_PLE), f32),
        'p_sample': jax.random.normal(ks[6], (DEPTH, DEC_BATCH, DEC_SEQ, D_PLE), f32),
        'w_pool': nrm(ks[7], (N_A, N_POOL_GROUPS, POOL_GROUP, POOL_GROUP), POOL_GROUP, BETA),
        'pool_scale': 1.0 + 0.1 * jax.random.normal(ks[8], (N_A, D_MODEL), f32),
        'w_kv': nrm(ks[9], (D_MODEL, 2 * N_HEADS * HEAD_DIM), D_MODEL),
        'w_q': nrm(ks[10], (N_B, D_MODEL, N_DIL * N_HEADS * HEAD_DIM), D_MODEL),
        'w_o': nrm(ks[11], (N_B, N_HEADS * HEAD_DIM, D_MODEL), N_HEADS * HEAD_DIM, BETA),
        'ln_g': 1.0 + 0.05 * jax.random.normal(ks[12], (DEPTH, 3, D_MODEL), f32),
        'ln_b': 0.02 * jax.random.normal(ks[13], (DEPTH, 3, D_MODEL), f32),
        'w_router': nrm(ks[14], (D_MODEL, N_EXPERTS), D_MODEL),
        'b_router': 0.01 * jax.random.normal(ks[15], (N_EXPERTS,), f32),
        'w_exp_gate': nrm(ks[16], (DEPTH, N_EXPERTS, D_MODEL, D_EXPERT), D_MODEL),
        'w_exp_up': nrm(ks[17], (DEPTH, N_EXPERTS, D_MODEL, D_EXPERT), D_MODEL),
        'w_exp_down': nrm(ks[18], (DEPTH, N_EXPERTS, D_EXPERT, D_MODEL), D_EXPERT, BETA),
        'w_ple_gate': nrm(ks[19], (DEPTH, D_MODEL, D_MODEL), D_MODEL),
        'w_ple_proj': nrm(ks[20], (DEPTH, D_PLE, D_MODEL), D_PLE, BETA),
    }


def reference(x_prompt, x_sample, state_pool, cache_k, cache_v, p_prompt, p_sample,
              w_pool, pool_scale, w_kv, w_q, w_o, ln_g, ln_b, w_router, b_router,
              w_exp_gate, w_exp_up, w_exp_down, w_ple_gate, w_ple_proj):
    B, T, D = x_prompt.shape
    y_prompt, pool_prompt, k_p, v_p = trunk(
        x_prompt, p_prompt, jnp.zeros((N_A, B, POOL_HIST, D), x_prompt.dtype), 0.0, None, None, 0,
        w_pool, pool_scale, w_kv, w_q, w_o, ln_g, ln_b, w_router, b_router,
        w_exp_gate, w_exp_up, w_exp_down, w_ple_gate, w_ple_proj)
    y_sample, pool_sample, k_s, v_s = trunk(
        x_sample, p_sample, state_pool, 1.0, cache_k, cache_v, PAST_LEN,
        w_pool, pool_scale, w_kv, w_q, w_o, ln_g, ln_b, w_router, b_router,
        w_exp_gate, w_exp_up, w_exp_down, w_ple_gate, w_ple_proj)
    keep_p = max(T - W_MAX, 0)
    k_prompt = k_p[:, keep_p:]
    v_prompt = v_p[:, keep_p:]
    w_buf = cache_k.shape[1]
    k_sample = jnp.concatenate([cache_k.astype(k_s.dtype), k_s], axis=1)[:, -w_buf:]
    v_sample = jnp.concatenate([cache_v.astype(v_s.dtype), v_s], axis=1)[:, -w_buf:]
    return (y_prompt, y_sample, pool_prompt, pool_sample, k_prompt, v_prompt, k_sample, v_sample)
```

```python
import functools

import numpy as np
import jax
import jax.numpy as jnp
from jax import lax
from jax.experimental import pallas as pl
from jax.experimental.pallas import tpu as pltpu

D_MODEL = 1024
DEPTH = 4
N_A = DEPTH // 2
N_B = DEPTH - N_A
POOL_WINDOWS = (2, 4, 8, 16)
POOL_GROUP = D_MODEL // len(POOL_WINDOWS)
POOL_HIST = max(POOL_WINDOWS) - 1
HEAD_DIM = 64
N_HEADS = D_MODEL // HEAD_DIM
DIL_PATTERNS = ((128, 1), (512, 4), (2048, 16))
N_DIL = len(DIL_PATTERNS)
W_MAX = 2048
N_EXPERTS = 16
N_EXPERT_GROUPS = 4
EXPERTS_PER_GROUP = 4
D_EXPERT = 512
D_PLE = 256
ALPHA = (2 * DEPTH) ** 0.25
LN_EPS = 1e-5

PAIRS = ((0, 1), (0, 2), (0, 3), (1, 2), (1, 3), (2, 3))
N_CLASSES = N_EXPERT_GROUPS * len(PAIRS)

LANES = 128
HALO = 16
TT = 512
TM = 256
QB = 128
ROW_W = D_MODEL + D_PLE + LANES
GATHER_CHUNK = 512
VMEM_LIMIT = 48 * 1024 * 1024
NEG = -1e30

F32 = jnp.float32
BF16 = jnp.bfloat16
NT_DIMS = (((1,), (1,)), ((), ()))


def _alibi_slopes():
    n = N_DIL * N_HEADS
    return (2.0 ** (-8.0 * np.arange(1, n + 1) / n)).astype(np.float32).reshape(N_DIL, N_HEADS)


SLOPES = _alibi_slopes()


def _ln(x, g, b):
    mu = jnp.mean(x, axis=-1, keepdims=True)
    xc = x - mu
    var = jnp.mean(xc * xc, axis=-1, keepdims=True)
    return xc * lax.rsqrt(var + LN_EPS) * g + b


def _split_bf16(x):
    hi = x.astype(BF16)
    lo = (x - hi.astype(F32)).astype(BF16)
    return hi, lo


def _route(x1, wrt_ref, br_ref):
    xh, xl = _split_bf16(x1)
    r = lax.dot_general(wrt_ref[...], xh, NT_DIMS, preferred_element_type=F32)
    r2 = lax.dot_general(wrt_ref[0:N_EXPERTS, :], xl, NT_DIMS, preferred_element_type=F32)
    z = r[0:N_EXPERTS] + r[N_EXPERTS:2 * N_EXPERTS] + r2
    s = jax.nn.sigmoid(z)
    sel = s + br_ref[...]
    a = [sel[k:k + 1, :] for k in range(N_EXPERTS)]
    sv = [s[k:k + 1, :] for k in range(N_EXPERTS)]
    one = jnp.ones_like(a[0])
    zero = jnp.zeros_like(a[0])
    gscore, chosen = [], []
    for g in range(N_EXPERT_GROUPS):
        ag = a[4 * g:4 * g + 4]
        cnt = []
        for j in range(4):
            c = zero
            for k in range(4):
                if k == j:
                    continue
                beats = (ag[k] > ag[j]) | (ag[k] == ag[j]) if k < j else (ag[k] > ag[j])
                c = c + jnp.where(beats, one, zero)
            cnt.append(c)
        top = zero
        sec = zero
        for j in range(4):
            top = top + jnp.where(cnt[j] == 0.0, ag[j], zero)
            sec = sec + jnp.where(cnt[j] == 1.0, ag[j], zero)
        gscore.append(top + sec)
        chosen.append([jnp.where(cnt[j] < 2.0, one, zero) for j in range(4)])
    best = gscore[0]
    gi = zero
    for g in range(1, N_EXPERT_GROUPS):
        better = gscore[g] > best
        gi = jnp.where(better, float(g), gi)
        best = jnp.where(better, gscore[g], best)
    m = []
    sg = []
    for j in range(4):
        mj = chosen[0][j]
        sj = sv[j]
        for g in range(1, N_EXPERT_GROUPS):
            isg = gi == float(g)
            mj = jnp.where(isg, chosen[g][j], mj)
            sj = jnp.where(isg, sv[4 * g + j], sj)
        m.append(mj)
        sg.append(sj)
    pidx = zero
    s_lo = zero
    s_hi = zero
    for idx, (p0, p1) in enumerate(PAIRS):
        hit = (m[p0] * m[p1]) > 0.5
        pidx = jnp.where(hit, float(idx), pidx)
        s_lo = jnp.where(hit, sg[p0], s_lo)
        s_hi = jnp.where(hit, sg[p1], s_hi)
    den = s_lo + s_hi
    cls = (gi * float(len(PAIRS)) + pidx).astype(jnp.int32)
    return s_lo / den, s_hi / den, cls


def _tail(x, mix, p, lng, lnb, wrt_ref, br_ref, r_ref, cls_ref):
    x1 = _ln(ALPHA * x + mix, lng, lnb)
    g_lo, g_hi, cls = _route(x1, wrt_ref, br_ref)
    n = x.shape[0]
    row = lax.broadcasted_iota(jnp.int32, (LANES, n), 0)
    meta_t = jnp.where(row == 0, g_lo, jnp.where(row == 1, g_hi, 0.0))
    r_ref[:, 0:D_MODEL] = x1
    r_ref[:, D_MODEL:D_MODEL + D_PLE] = p
    r_ref[:, D_MODEL + D_PLE:ROW_W] = meta_t.T
    cls_ref[...] = cls


def _pool_kernel(x_ref, halo_ref, exts_ref, p_ref, wp_ref, sc_ref, lng_ref, lnb_ref, wrt_ref, br_ref,
                 r_ref, cls_ref, ext_scr, *, n_prompt_steps, steps_per_seq):
    s = pl.program_id(0)

    def finish(x, diff_parts, p):
        ys = [jnp.dot(diff_parts[g].astype(BF16), wp_ref[g], preferred_element_type=F32)
              for g in range(len(POOL_WINDOWS))]
        mix = jnp.concatenate(ys, axis=1) * sc_ref[...]
        _tail(x, mix, p, lng_ref[...], lnb_ref[...], wrt_ref, br_ref, r_ref, cls_ref)

    @pl.when(s < n_prompt_steps)
    def _():
        t = s % steps_per_seq
        keep = jnp.where(t > 0, 1.0, 0.0)
        ext_scr[0:HALO, :] = halo_ref[...] * keep
        ext_scr[HALO:HALO + TT, :] = x_ref[...]
        tpos = (t * TT + lax.broadcasted_iota(jnp.int32, (TT, 1), 0)).astype(F32)
        parts = []
        for g, w in enumerate(POOL_WINDOWS):
            cols = pl.ds(g * POOL_GROUP, POOL_GROUP)
            xg = ext_scr[pl.ds(HALO, TT), cols]
            acc = xg
            for k in range(1, w):
                acc = acc + ext_scr[pl.ds(HALO - k, TT), cols]
            cnt = jnp.minimum(float(w), tpos + 1.0)
            parts.append(acc / cnt - xg)
        finish(x_ref[...], parts, p_ref[...])

    @pl.when(s >= n_prompt_steps)
    def _():
        nb = exts_ref.shape[0]
        hist = exts_ref.shape[1] - HALO
        parts = []
        for g, w in enumerate(POOL_WINDOWS):
            cols = pl.ds(g * POOL_GROUP, POOL_GROUP)
            xg = exts_ref[:, pl.ds(HALO, hist), cols]
            acc = xg
            for k in range(1, w):
                acc = acc + exts_ref[:, pl.ds(HALO - k, hist), cols]
            parts.append((acc / float(w) - xg).reshape(nb * hist, POOL_GROUP))
        x = exts_ref[:, pl.ds(HALO, hist), :].reshape(nb * hist, D_MODEL)
        finish(x, parts, p_ref[...])


def _pool_layer(xp_flat, ext_s, p_flat, wp, sc, lng, lnb, wrt, br, *, n_prompt, seq_len):
    ntok = p_flat.shape[0]
    n_steps = ntok // TT
    npst = n_prompt // TT
    sps = seq_len // TT
    dec_seq = ext_s.shape[1] - HALO
    nb = TT // dec_seq
    kern = functools.partial(_pool_kernel, n_prompt_steps=npst, steps_per_seq=sps)
    const = lambda s: (0, 0)
    return pl.pallas_call(
        kern,
        grid=(n_steps,),
        in_specs=[
            pl.BlockSpec((TT, D_MODEL), lambda s: (jnp.minimum(s, npst - 1), 0)),
            pl.BlockSpec((HALO, D_MODEL),
                         lambda s: (jnp.maximum(jnp.minimum(s, npst - 1) * (TT // HALO) - 1, 0), 0)),
            pl.BlockSpec((nb, HALO + dec_seq, D_MODEL), lambda s: (jnp.maximum(s - npst, 0), 0, 0)),
            pl.BlockSpec((TT, D_PLE), lambda s: (s, 0)),
            pl.BlockSpec((len(POOL_WINDOWS), POOL_GROUP, POOL_GROUP), lambda s: (0, 0, 0)),
            pl.BlockSpec((1, D_MODEL), const),
            pl.BlockSpec((1, D_MODEL), const),
            pl.BlockSpec((1, D_MODEL), const),
            pl.BlockSpec((2 * N_EXPERTS, D_MODEL), const),
            pl.BlockSpec((N_EXPERTS, 1), const),
        ],
        out_specs=[
            pl.BlockSpec((TT, ROW_W), lambda s: (s, 0)),
            pl.BlockSpec((1, TT), lambda s: (0, s)),
        ],
        out_shape=[
            jax.ShapeDtypeStruct((ntok, ROW_W), F32),
            jax.ShapeDtypeStruct((1, ntok), jnp.int32),
        ],
        scratch_shapes=[pltpu.VMEM((HALO + TT, D_MODEL), F32)],
        compiler_params=pltpu.CompilerParams(dimension_semantics=("arbitrary",),
                                             vmem_limit_bytes=VMEM_LIMIT),
        name="pool_mixer",
    )(xp_flat, xp_flat, ext_s, p_flat, wp, sc, lng, lnb, wrt, br)


def _gather_kernel(idx_ref, src_ref, out_ref, sem):
    base = pl.program_id(0) * GATHER_CHUNK

    def issue(j, carry):
        r = idx_ref[base + j]
        pltpu.make_async_copy(src_ref.at[pl.ds(r, 1)], out_ref.at[pl.ds(base + j, 1)], sem).start()
        return carry

    lax.fori_loop(0, GATHER_CHUNK, issue, 0)
    pltpu.make_async_copy(src_ref.at[pl.ds(0, GATHER_CHUNK)], out_ref.at[pl.ds(base, GATHER_CHUNK)], sem).wait()


def _gather_rows(src, idx):
    n_out = idx.shape[0]
    return pl.pallas_call(
        _gather_kernel,
        grid_spec=pltpu.PrefetchScalarGridSpec(
            num_scalar_prefetch=1,
            grid=(n_out // GATHER_CHUNK,),
            in_specs=[pl.BlockSpec(memory_space=pl.ANY)],
            out_specs=pl.BlockSpec(memory_space=pl.ANY),
            scratch_shapes=[pltpu.SemaphoreType.DMA(())],
        ),
        out_shape=jax.ShapeDtypeStruct((n_out, src.shape[1]), src.dtype),
        compiler_params=pltpu.CompilerParams(dimension_semantics=("arbitrary",)),
        name="row_gather",
    )(idx, src)


def _moe_kernel(elo_ref, ehi_ref, valid_ref, r_ref, wg_lo, wu_lo, wd_lo, wg_hi, wu_hi, wd_hi,
                wpg_ref, wpp_ref, lng_ref, lnb_ref, y_ref):
    k = pl.program_id(0)

    @pl.when(valid_ref[k] == 0)
    def _():
        y_ref[...] = jnp.zeros_like(y_ref)

    @pl.when(valid_ref[k] != 0)
    def _():
        x1 = r_ref[:, 0:D_MODEL]
        p = r_ref[:, D_MODEL:D_MODEL + D_PLE]
        gates = r_ref[:, D_MODEL + D_PLE:ROW_W]
        xb = x1.astype(BF16)

        def expert(wg, wu, wd):
            hg = jnp.dot(xb, wg[...], preferred_element_type=F32)
            hu = jnp.dot(xb, wu[...], preferred_element_type=F32)
            h = (hg * jax.nn.sigmoid(hg)) * hu
            return jnp.dot(h.astype(BF16), wd[...], preferred_element_type=F32)

        moe = gates[:, 0:1] * expert(wg_lo, wu_lo, wd_lo)
        moe = moe + gates[:, 1:2] * expert(wg_hi, wu_hi, wd_hi)
        x2 = _ln(ALPHA * x1 + moe, lng_ref[0:1, :], lnb_ref[0:1, :])
        gate = jax.nn.sigmoid(jnp.dot(x2.astype(BF16), wpg_ref[...], preferred_element_type=F32))
        proj = jnp.dot(p.astype(BF16), wpp_ref[...], preferred_element_type=F32)
        y_ref[...] = _ln(ALPHA * x2 + gate * proj, lng_ref[1:2, :], lnb_ref[1:2, :])


def _moe_layer(rs, elo, ehi, valid, wg, wu, wd, wpg, wpp, lng2, lnb2, *, layer):
    ns = rs.shape[0]
    n_tiles = ns // TM
    lo = lambda k, elo, ehi, valid: (layer, elo[k], 0, 0)
    hi = lambda k, elo, ehi, valid: (layer, ehi[k], 0, 0)
    const2 = lambda k, elo, ehi, valid: (0, 0)
    gu_spec = lambda im: pl.BlockSpec((None, None, D_MODEL, D_EXPERT), im)
    dn_spec = lambda im: pl.BlockSpec((None, None, D_EXPERT, D_MODEL), im)
    return pl.pallas_call(
        _moe_kernel,
        grid_spec=pltpu.PrefetchScalarGridSpec(
            num_scalar_prefetch=3,
            grid=(n_tiles,),
            in_specs=[
                pl.BlockSpec((TM, ROW_W), lambda k, elo, ehi, valid: (k, 0)),
                gu_spec(lo), gu_spec(lo), dn_spec(lo),
                gu_spec(hi), gu_spec(hi), dn_spec(hi),
                pl.BlockSpec((D_MODEL, D_MODEL), const2),
                pl.BlockSpec((D_PLE, D_MODEL), const2),
                pl.BlockSpec((2, D_MODEL), const2),
                pl.BlockSpec((2, D_MODEL), const2),
            ],
            out_specs=pl.BlockSpec((TM, D_MODEL), lambda k, elo, ehi, valid: (k, 0)),
        ),
        out_shape=jax.ShapeDtypeStruct((ns, D_MODEL), F32),
        compiler_params=pltpu.CompilerParams(dimension_semantics=("arbitrary",),
                                             vmem_limit_bytes=VMEM_LIMIT),
        name="moe_ple",
    )(elo, ehi, valid, rs, wg, wu, wd, wg, wu, wd, wpg, wpp, lng2, lnb2)


def _route_plan(cls):
    ntok = cls.shape[0]
    n_tiles = -(-(ntok + N_CLASSES * (TM - 1)) // TM)
    ns = -(-n_tiles * TM // GATHER_CHUNK) * GATHER_CHUNK
    n_tiles = ns // TM
    onehot = (cls[:, None] == jnp.arange(N_CLASSES, dtype=jnp.int32)[None, :]).astype(jnp.int32)
    csum = jnp.cumsum(onehot, axis=0)
    rank = jnp.sum(csum * onehot, axis=1) - 1
    counts = csum[-1]
    padded = ((counts + TM - 1) // TM) * TM
    ends = jnp.cumsum(padded)
    starts = ends - padded
    pos = jnp.sum(onehot * starts[None, :], axis=1) + rank
    src = jnp.zeros((ns,), jnp.int32).at[pos].set(jnp.arange(ntok, dtype=jnp.int32))
    tile_start = jnp.arange(n_tiles, dtype=jnp.int32) * TM
    tile_cls = jnp.sum((tile_start[:, None] >= ends[None, :]).astype(jnp.int32), axis=1)
    valid = (tile_start < ends[-1]).astype(jnp.int32)
    last_cls = jnp.max(jnp.where(counts > 0, jnp.arange(N_CLASSES, dtype=jnp.int32), 0))
    tile_cls = jnp.minimum(tile_cls, last_cls)
    grp = tile_cls // len(PAIRS)
    pidx = tile_cls % len(PAIRS)
    p0 = jnp.asarray([p[0] for p in PAIRS], jnp.int32)[pidx]
    p1 = jnp.asarray([p[1] for p in PAIRS], jnp.int32)[pidx]
    return pos, src, grp * EXPERTS_PER_GROUP + p0, grp * EXPERTS_PER_GROUP + p1, valid


def _kv_kernel(x_ref, w_ref, kf_ref, vf_ref, kb_ref, vb_ref):
    kv = jnp.dot(x_ref[...].astype(BF16), w_ref[...], preferred_element_type=F32)
    k = kv[:, 0:D_MODEL]
    v = kv[:, D_MODEL:2 * D_MODEL]
    kf_ref[...] = k
    vf_ref[...] = v
    kb_ref[...] = k.astype(BF16)
    vb_ref[...] = v.astype(BF16)


def _kv_proj(x, w):
    ntok = x.shape[0]
    blk = lambda s: (s, 0)
    return pl.pallas_call(
        _kv_kernel,
        grid=(ntok // TT,),
        in_specs=[pl.BlockSpec((TT, D_MODEL), blk), pl.BlockSpec((D_MODEL, 2 * D_MODEL), lambda s: (0, 0))],
        out_specs=[pl.BlockSpec((TT, D_MODEL), blk)] * 4,
        out_shape=[jax.ShapeDtypeStruct((ntok, D_MODEL), F32)] * 2
        + [jax.ShapeDtypeStruct((ntok, D_MODEL), BF16)] * 2,
        compiler_params=pltpu.CompilerParams(dimension_semantics=("arbitrary",),
                                             vmem_limit_bytes=VMEM_LIMIT),
        name="kv_proj",
    )(x, w)


def _q_kernel(x_ref, w_ref, q_ref):
    q = jnp.dot(x_ref[...].astype(BF16), w_ref[...], preferred_element_type=F32)
    q_ref[...] = (q * (HEAD_DIM ** -0.5)).astype(q_ref.dtype)


def _q_proj(x, w):
    ntok = x.shape[0]
    n_out = w.shape[1]
    return pl.pallas_call(
        _q_kernel,
        grid=(ntok // TT, n_out // D_MODEL),
        in_specs=[pl.BlockSpec((TT, D_MODEL), lambda s, c: (s, 0)),
                  pl.BlockSpec((D_MODEL, D_MODEL), lambda s, c: (0, c))],
        out_specs=pl.BlockSpec((TT, D_MODEL), lambda s, c: (s, c)),
        out_shape=jax.ShapeDtypeStruct((ntok, n_out), BF16),
        compiler_params=pltpu.CompilerParams(dimension_semantics=("arbitrary", "arbitrary"),
                                             vmem_limit_bytes=VMEM_LIMIT),
        name="q_proj",
    )(x, w)


def _attn_prompt_kernel(q_ref, kp_ref, kc_ref, vp_ref, vc_ref, o_ref, lse_ref, *, group):
    i = pl.program_id(2)
    dil = DIL_PATTERNS[group][1]
    a = lax.broadcasted_iota(jnp.int32, (QB, 2 * QB), 0)
    j = lax.broadcasted_iota(jnp.int32, (QB, 2 * QB), 1)
    delta = QB + a - j
    ok = (delta >= 0) & (delta <= QB) & ((j >= QB) | (i > 0))
    dist = (delta * dil).astype(F32)
    lane = lax.broadcasted_iota(jnp.int32, (QB, LANES), 1)
    first_half = lane < HEAD_DIM
    lse_blk = jnp.zeros((QB, LANES), F32)
    for hp in range(N_HEADS // 2):
        cols = pl.ds(hp * LANES, LANES)
        q2 = q_ref[:, cols]
        k2 = jnp.concatenate([kp_ref[:, cols], kc_ref[:, cols]], axis=0)
        v2 = jnp.concatenate([vp_ref[:, cols], vc_ref[:, cols]], axis=0)
        outs = []
        for half in range(2):
            h = 2 * hp + half
            in_head = first_half if half == 0 else jnp.logical_not(first_half)
            qm = jnp.where(in_head, q2, jnp.zeros_like(q2))
            s = lax.dot_general(qm, k2, NT_DIMS, preferred_element_type=F32)
            s = jnp.where(ok, s - float(SLOPES[group, h]) * dist, NEG)
            mx = jnp.max(s, axis=1, keepdims=True)
            pexp = jnp.exp(s - mx)
            den = jnp.sum(pexp, axis=1, keepdims=True)
            pv = jnp.dot(pexp.astype(BF16), v2, preferred_element_type=F32)
            outs.append(pv / den)
            lse_blk = jnp.where(lane == h, mx + jnp.log(den), lse_blk)
        o_ref[:, cols] = jnp.where(first_half, outs[0], outs[1]).astype(o_ref.dtype)
    lse_ref[...] = lse_blk


def _attn_prompt(q, kb, vb, *, group, batch, seq_len):
    dil = DIL_PATTERNS[group][1]
    rows = seq_len // dil
    nq = rows // QB
    qv = q.reshape(batch, rows, dil * N_DIL * D_MODEL)
    kv = kb.reshape(batch, rows, dil * D_MODEL)
    vv = vb.reshape(batch, rows, dil * D_MODEL)
    cur = lambda b, r, i: (b, i, r)
    prev = lambda b, r, i: (b, jnp.maximum(i - 1, 0), r)
    blk = lambda im: pl.BlockSpec((None, QB, D_MODEL), im)
    o, lse = pl.pallas_call(
        functools.partial(_attn_prompt_kernel, group=group),
        grid=(batch, dil, nq),
        in_specs=[pl.BlockSpec((None, QB, D_MODEL), lambda b, r, i: (b, i, r * N_DIL + group)),
                  blk(prev), blk(cur), blk(prev), blk(cur)],
        out_specs=[blk(cur), pl.BlockSpec((None, QB, LANES), cur)],
        out_shape=[jax.ShapeDtypeStruct((batch, rows, dil * D_MODEL), BF16),
                   jax.ShapeDtypeStruct((batch, rows, dil * LANES), F32)],
        compiler_params=pltpu.CompilerParams(dimension_semantics=("arbitrary",) * 3,
                                             vmem_limit_bytes=VMEM_LIMIT),
        name=f"attn_prompt_g{group}",
    )(qv, kv, kv, vv, vv)
    return o.reshape(batch * seq_len, D_MODEL), lse.reshape(batch * seq_len, LANES)


def _attn_sample_kernel(q_ref, c2k_ref, c2v_ref, c1k_ref, c1v_ref, kn_ref, vn_ref, sl_ref, o_ref,
                        *, dec_seq):
    nrow = dec_seq * N_HEADS
    rr = lax.broadcasted_iota(jnp.int32, (nrow, D_MODEL), 0)
    cc = lax.broadcasted_iota(jnp.int32, (nrow, D_MODEL), 1)
    head_mask = (cc // HEAD_DIM) == (rr % N_HEADS)
    pad_rows = 16 - dec_seq
    kn = jnp.concatenate([kn_ref[...], jnp.zeros((pad_rows, D_MODEL), F32)], axis=0).astype(BF16)
    vn = jnp.concatenate([vn_ref[...], jnp.zeros((pad_rows, D_MODEL), F32)], axis=0).astype(BF16)
    tq_n = lax.broadcasted_iota(jnp.int32, (nrow, 16), 0) // N_HEADS
    tk_n = lax.broadcasted_iota(jnp.int32, (nrow, 16), 1)

    def expand_q(g):
        qg = q_ref[:, pl.ds(g * D_MODEL, D_MODEL)].astype(F32)
        qrep = jnp.broadcast_to(qg[:, None, :], (dec_seq, N_HEADS, D_MODEL)).reshape(nrow, D_MODEL)
        return jnp.where(head_mask, qrep, 0.0).astype(BF16)

    def softmax_pv(s_c, pv_fn, s_n):
        mx = jnp.maximum(jnp.max(s_c, axis=1, keepdims=True), jnp.max(s_n, axis=1, keepdims=True))
        p_c = jnp.exp(s_c - mx)
        p_n = jnp.exp(s_n - mx)
        den = jnp.sum(p_c, axis=1, keepdims=True) + jnp.sum(p_n, axis=1, keepdims=True)
        acc = pv_fn(p_c.astype(BF16)) + jnp.dot(p_n.astype(BF16), vn, preferred_element_type=F32)
        return acc / den, mx + jnp.log(den)

    outs, lses = [], []
    for g, (w, dil) in enumerate(DIL_PATTERNS):
        qe = expand_q(g)
        slope = sl_ref[g]
        s_n = lax.dot_general(qe, kn, NT_DIMS, preferred_element_type=F32)
        dn = tq_n - tk_n
        ok_n = (dn >= 0) & ((dn & (dil - 1)) == 0) & (tk_n < dec_seq)
        s_n = jnp.where(ok_n, s_n - slope * dn.astype(F32), NEG)
        if dil < dec_seq or w < W_MAX:
            tail = c1k_ref.shape[0]
            kc = c1k_ref[pl.ds(tail - w, w), :].astype(BF16)
            vc = c1v_ref[pl.ds(tail - w, w), :].astype(BF16)
            s_c = lax.dot_general(qe, kc, NT_DIMS, preferred_element_type=F32)
            tq = lax.broadcasted_iota(jnp.int32, (nrow, w), 0) // N_HEADS
            jk = lax.broadcasted_iota(jnp.int32, (nrow, w), 1)
            dc = w + tq - jk
            ok_c = (dc <= w) & (((jk - tq) & (dil - 1)) == 0)
            s_c = jnp.where(ok_c, s_c - slope * dc.astype(F32), NEG)
            o_g, lse_g = softmax_pv(s_c, lambda pb: jnp.dot(pb, vc, preferred_element_type=F32), s_n)
        else:
            n_m = c2k_ref.shape[0]
            pieces = []
            for t in range(dec_seq):
                kt = c2k_ref[:, pl.ds(t * D_MODEL, D_MODEL)].astype(BF16)
                pieces.append(lax.dot_general(qe[t * N_HEADS:(t + 1) * N_HEADS], kt, NT_DIMS,
                                              preferred_element_type=F32))
            s_c = jnp.concatenate(pieces, axis=0)
            mk = lax.broadcasted_iota(jnp.int32, (nrow, n_m), 1)
            dc = (W_MAX - mk * dil).astype(F32)
            s_c = s_c - slope * dc

            def pv_fn(pb):
                res = []
                for t in range(dec_seq):
                    vt = c2v_ref[:, pl.ds(t * D_MODEL, D_MODEL)].astype(BF16)
                    res.append(jnp.dot(pb[t * N_HEADS:(t + 1) * N_HEADS], vt, preferred_element_type=F32))
                return jnp.concatenate(res, axis=0)

            o_g, lse_g = softmax_pv(s_c, pv_fn, s_n)
        outs.append(o_g)
        lses.append(lse_g)
    top = jnp.maximum(jnp.maximum(lses[0], lses[1]), lses[2])
    ws = [jnp.exp(l - top) for l in lses]
    tot = ws[0] + ws[1] + ws[2]
    mixed = (ws[0] * outs[0] + ws[1] * outs[1] + ws[2] * outs[2]) / tot
    mixed = jnp.where(head_mask, mixed, 0.0).reshape(dec_seq, N_HEADS, D_MODEL)
    o_ref[...] = jnp.sum(mixed, axis=1)


def _attn_sample(q_s, cache_k, cache_v, kn, vn, slope_rows, *, dec_seq):
    nb = q_s.shape[0]
    dil2 = DIL_PATTERNS[2][1]
    n_m = W_MAX // dil2
    c2k = cache_k.reshape(nb, n_m, dil2 * D_MODEL)
    c2v = cache_v.reshape(nb, n_m, dil2 * D_MODEL)
    tail = DIL_PATTERNS[1][0]
    c2_spec = pl.BlockSpec((None, n_m, dec_seq * D_MODEL), lambda b: (b, 0, 0))
    c1_spec = pl.BlockSpec((None, tail, D_MODEL), lambda b: (b, W_MAX // tail - 1, 0))
    new_spec = pl.BlockSpec((None, dec_seq, D_MODEL), lambda b: (b, 0, 0))
    return pl.pallas_call(
        functools.partial(_attn_sample_kernel, dec_seq=dec_seq),
        grid=(nb,),
        in_specs=[pl.BlockSpec((None, dec_seq, N_DIL * D_MODEL), lambda b: (b, 0, 0)),
                  c2_spec, c2_spec, c1_spec, c1_spec, new_spec, new_spec,
                  pl.BlockSpec((N_DIL, dec_seq * N_HEADS, 1), lambda b: (0, 0, 0))],
        out_specs=pl.BlockSpec((None, dec_seq, D_MODEL), lambda b: (b, 0, 0)),
        out_shape=jax.ShapeDtypeStruct((nb, dec_seq, D_MODEL), F32),
        compiler_params=pltpu.CompilerParams(dimension_semantics=("arbitrary",),
                                             vmem_limit_bytes=VMEM_LIMIT),
        name="attn_sample",
    )(q_s, c2k, c2v, cache_k, cache_v, kn, vn, slope_rows)


def _attn_out_kernel(x_ref, o0_ref, o1_ref, o2_ref, l0_ref, l1_ref, l2_ref, os_ref, e_ref, wo_ref, p_ref,
                     lng_ref, lnb_ref, wrt_ref, br_ref, r_ref, cls_ref, *, n_prompt_steps):
    s = pl.program_id(0)

    def finish(o):
        mix = jnp.dot(o.astype(BF16), wo_ref[...], preferred_element_type=F32)
        _tail(x_ref[...], mix, p_ref[...], lng_ref[...], lnb_ref[...], wrt_ref, br_ref, r_ref, cls_ref)

    @pl.when(s < n_prompt_steps)
    def _():
        lses = [l0_ref[...], l1_ref[...], l2_ref[...]]
        top = jnp.maximum(jnp.maximum(lses[0], lses[1]), lses[2])
        ws = [jnp.exp(l - top) for l in lses]
        tot = ws[0] + ws[1] + ws[2]
        o = jnp.zeros((TT, D_MODEL), F32)
        for wgt, o_ref in zip(ws, (o0_ref, o1_ref, o2_ref)):
            hi, lo = _split_bf16(wgt / tot)
            wide = (jnp.dot(hi, e_ref[...], preferred_element_type=F32)
                    + jnp.dot(lo, e_ref[...], preferred_element_type=F32))
            o = o + wide * o_ref[...].astype(F32)
        finish(o)

    @pl.when(s >= n_prompt_steps)
    def _():
        finish(os_ref[...])


def _attn_out_layer(x, o3, l3, o_s, expand, wo, p_flat, lng, lnb, wrt, br, *, n_prompt):
    ntok = x.shape[0]
    npst = n_prompt // TT
    blk = lambda s: (s, 0)
    pblk = lambda s: (jnp.minimum(s, npst - 1), 0)
    sblk = lambda s: (jnp.maximum(s - npst, 0), 0)
    const = lambda s: (0, 0)
    return pl.pallas_call(
        functools.partial(_attn_out_kernel, n_prompt_steps=npst),
        grid=(ntok // TT,),
        in_specs=[pl.BlockSpec((TT, D_MODEL), blk)]
        + [pl.BlockSpec((TT, D_MODEL), pblk)] * 3
        + [pl.BlockSpec((TT, LANES), pblk)] * 3
        + [pl.BlockSpec((TT, D_MODEL), sblk),
           pl.BlockSpec((LANES, D_MODEL), const),
           pl.BlockSpec((D_MODEL, D_MODEL), const),
           pl.BlockSpec((TT, D_PLE), blk),
           pl.BlockSpec((1, D_MODEL), const),
           pl.BlockSpec((1, D_MODEL), const),
           pl.BlockSpec((2 * N_EXPERTS, D_MODEL), const),
           pl.BlockSpec((N_EXPERTS, 1), const)],
        out_specs=[pl.BlockSpec((TT, ROW_W), blk), pl.BlockSpec((1, TT), lambda s: (0, s))],
        out_shape=[jax.ShapeDtypeStruct((ntok, ROW_W), F32), jax.ShapeDtypeStruct((1, ntok), jnp.int32)],
        compiler_params=pltpu.CompilerParams(dimension_semantics=("arbitrary",),
                                             vmem_limit_bytes=VMEM_LIMIT),
        name="attn_out",
    )(x, *o3, *l3, o_s, expand, wo, p_flat, lng, lnb, wrt, br)


def _shift_kernel(ck_ref, cv_ref, kn_ref, vn_ref, ok_ref, ov_ref, sem, *, dec_seq, per_step):
    b0 = pl.program_id(0) * per_step
    keep = ck_ref.shape[1] - dec_seq
    copies = []
    for src, new, dst in ((ck_ref, kn_ref, ok_ref), (cv_ref, vn_ref, ov_ref)):
        copies.append(pltpu.make_async_copy(src.at[pl.ds(b0, per_step), pl.ds(dec_seq, keep)],
                                            dst.at[pl.ds(b0, per_step), pl.ds(0, keep)], sem.at[len(copies)]))
        copies.append(pltpu.make_async_copy(new.at[pl.ds(b0, per_step)],
                                            dst.at[pl.ds(b0, per_step), pl.ds(keep, dec_seq)],
                                            sem.at[len(copies)]))
    for c in copies:
        c.start()
    for c in copies:
        c.wait()


def _shift_cache(cache_k, cache_v, kn, vn):
    nb, wbuf, _ = cache_k.shape
    dec_seq = kn.shape[1]
    per_step = 8
    anyspec = pl.BlockSpec(memory_space=pl.ANY)
    return pl.pallas_call(
        functools.partial(_shift_kernel, dec_seq=dec_seq, per_step=per_step),
        grid=(nb // per_step,),
        in_specs=[anyspec] * 4,
        out_specs=[anyspec] * 2,
        out_shape=[jax.ShapeDtypeStruct(cache_k.shape, cache_k.dtype)] * 2,
        scratch_shapes=[pltpu.SemaphoreType.DMA((4,))],
        compiler_params=pltpu.CompilerParams(dimension_semantics=("arbitrary",)),
        name="kv_window_shift",
    )(cache_k, cache_v, kn, vn)


def kernel(x_prompt, x_sample, state_pool, cache_k, cache_v, p_prompt, p_sample, w_pool, pool_scale, w_kv,
           w_q, w_o, ln_g, ln_b, w_router, b_router, w_exp_gate, w_exp_up, w_exp_down, w_ple_gate,
           w_ple_proj):
    batch, seq_len, d = x_prompt.shape
    nb, dec_seq, _ = x_sample.shape
    wbuf = cache_k.shape[1]
    n_prompt = batch * seq_len
    n_sample = nb * dec_seq
    ntok = n_prompt + n_sample
    assert d == D_MODEL and wbuf == W_MAX and dec_seq == 8
    assert seq_len % (QB * DIL_PATTERNS[2][1]) == 0 and n_sample % TT == 0 and seq_len % TT == 0

    wrt = w_router.T.astype(F32)
    wrt_hi = wrt.astype(BF16)
    wrt_lo = (wrt - wrt_hi.astype(F32)).astype(BF16)
    wrt2 = jnp.concatenate([wrt_hi, wrt_lo], axis=0)
    br = b_router.astype(F32).reshape(N_EXPERTS, 1)
    wg, wu, wd = w_exp_gate.astype(BF16), w_exp_up.astype(BF16), w_exp_down.astype(BF16)
    wpg, wpp = w_ple_gate.astype(BF16), w_ple_proj.astype(BF16)
    wp = w_pool.astype(BF16)
    head_of_col = np.arange(D_MODEL) // HEAD_DIM
    expand = jnp.asarray(np.arange(LANES)[:, None] == head_of_col[None, :], BF16)
    slope_rows = jnp.asarray(np.tile(SLOPES[:, None, :], (1, dec_seq, 1)).reshape(N_DIL, dec_seq * N_HEADS, 1))

    p_flat = jnp.concatenate([p_prompt.reshape(DEPTH, n_prompt, D_PLE),
                              p_sample.reshape(DEPTH, n_sample, D_PLE)], axis=1)

    def moe_stage(rows, cls, layer):
        pos, src, elo, ehi, valid = _route_plan(cls.reshape(ntok))
        rs = _gather_rows(rows, src)
        ys = _moe_layer(rs, elo, ehi, valid, wg, wu, wd, wpg[layer], wpp[layer],
                        ln_g[layer, 1:3], ln_b[layer, 1:3], layer=layer)
        return _gather_rows(ys, pos)

    x = x_prompt.reshape(n_prompt, D_MODEL)
    xs = x_sample
    pool_p, pool_s = [], []
    for i in range(N_A):
        ext_s = jnp.concatenate([jnp.zeros((nb, HALO - POOL_HIST, D_MODEL), F32), state_pool[i], xs], axis=1)
        pool_p.append(x[:n_prompt].reshape(batch, seq_len, D_MODEL)[:, seq_len - POOL_HIST:])
        pool_s.append(ext_s[:, HALO + dec_seq - POOL_HIST:])
        rows, cls = _pool_layer(x, ext_s, p_flat[i], wp[i], pool_scale[i].reshape(1, D_MODEL),
                                ln_g[i, 0].reshape(1, D_MODEL), ln_b[i, 0].reshape(1, D_MODEL), wrt2, br,
                                n_prompt=n_prompt, seq_len=seq_len)
        x = moe_stage(rows, cls, i)
        xs = x[n_prompt:].reshape(nb, dec_seq, D_MODEL)

    kf, vf, kb, vb = _kv_proj(x, w_kv.astype(BF16))
    k_new_s = kf[n_prompt:].reshape(nb, dec_seq, D_MODEL)
    v_new_s = vf[n_prompt:].reshape(nb, dec_seq, D_MODEL)
    ck = cache_k.reshape(nb, wbuf, D_MODEL)
    cv = cache_v.reshape(nb, wbuf, D_MODEL)

    for jl in range(N_B):
        i = N_A + jl
        q = _q_proj(x, w_q[jl].astype(BF16))
        o3, l3 = [], []
        for g in range(N_DIL):
            o_g, l_g = _attn_prompt(q[:n_prompt], kb[:n_prompt], vb[:n_prompt], group=g, batch=batch,
                                    seq_len=seq_len)
            o3.append(o_g)
            l3.append(l_g)
        o_s = _attn_sample(q[n_prompt:].reshape(nb, dec_seq, N_DIL * D_MODEL), ck, cv, k_new_s, v_new_s,
                           slope_rows, dec_seq=dec_seq)
        rows, cls = _attn_out_layer(x, o3, l3, o_s.reshape(n_sample, D_MODEL), expand, w_o[jl].astype(BF16),
                                    p_flat[i], ln_g[i, 0].reshape(1, D_MODEL), ln_b[i, 0].reshape(1, D_MODEL),
                                    wrt2, br, n_prompt=n_prompt)
        x = moe_stage(rows, cls, i)

    k_samp, v_samp = _shift_cache(ck, cv, k_new_s, v_new_s)
    keep_p = max(seq_len - W_MAX, 0)
    hd = (N_HEADS, HEAD_DIM)
    k_prompt = kf[:n_prompt].reshape(batch, seq_len, *hd)[:, keep_p:]
    v_prompt = vf[:n_prompt].reshape(batch, seq_len, *hd)[:, keep_p:]
    return (x[:n_prompt].reshape(batch, seq_len, D_MODEL),
            x[n_prompt:].reshape(nb, dec_seq, D_MODEL),
            jnp.stack(pool_p, 0), jnp.stack(pool_s, 0),
            k_prompt, v_prompt,
            k_samp.reshape(nb, wbuf, *hd), v_samp.reshape(nb, wbuf, *hd))
```

```python
import functools

import numpy as np
import jax
import jax.numpy as jnp
from jax import lax
from jax.experimental import pallas as pl
from jax.experimental.pallas import tpu as pltpu

D_MODEL = 1024
DEPTH = 4
N_A = DEPTH // 2
N_B = DEPTH - N_A
POOL_WINDOWS = (2, 4, 8, 16)
POOL_GROUP = D_MODEL // len(POOL_WINDOWS)
POOL_HIST = max(POOL_WINDOWS) - 1
HEAD_DIM = 64
N_HEADS = D_MODEL // HEAD_DIM
DIL_PATTERNS = ((128, 1), (512, 4), (2048, 16))
N_DIL = len(DIL_PATTERNS)
W_MAX = 2048
N_EXPERTS = 16
N_EXPERT_GROUPS = 4
EXPERTS_PER_GROUP = 4
D_EXPERT = 512
D_PLE = 256
ALPHA = (2 * DEPTH) ** 0.25
LN_EPS = 1e-5

PAIRS = ((0, 1), (0, 2), (0, 3), (1, 2), (1, 3), (2, 3))
N_CLASSES = N_EXPERT_GROUPS * len(PAIRS)

LANES = 128
HALO = 16
TT = 512
TM = 256
QB = 128
ROW_W = D_MODEL + D_PLE + LANES
VMEM_LIMIT = 48 * 1024 * 1024
NEG = -1e30

F32 = jnp.float32
BF16 = jnp.bfloat16
NT_DIMS = (((1,), (1,)), ((), ()))


def _alibi_slopes():
    n = N_DIL * N_HEADS
    return (2.0 ** (-8.0 * np.arange(1, n + 1) / n)).astype(np.float32).reshape(N_DIL, N_HEADS)


SLOPES = _alibi_slopes()


def _ln(x, g, b):
    mu = jnp.mean(x, axis=-1, keepdims=True)
    xc = x - mu
    var = jnp.mean(xc * xc, axis=-1, keepdims=True)
    return xc * lax.rsqrt(var + LN_EPS) * g + b


def _split_bf16(x):
    hi = x.astype(BF16)
    lo = (x - hi.astype(F32)).astype(BF16)
    return hi, lo


def _route(x1, wrt_ref, br_ref):
    z = lax.dot_general(wrt_ref[...], x1.astype(BF16), NT_DIMS, preferred_element_type=F32)
    s = jax.nn.sigmoid(z)
    sel = s + br_ref[...]
    a = [sel[k:k + 1, :] for k in range(N_EXPERTS)]
    sv = [s[k:k + 1, :] for k in range(N_EXPERTS)]
    one = jnp.ones_like(a[0])
    zero = jnp.zeros_like(a[0])
    gscore, chosen = [], []
    for g in range(N_EXPERT_GROUPS):
        ag = a[4 * g:4 * g + 4]
        cnt = []
        for j in range(4):
            c = zero
            for k in range(4):
                if k == j:
                    continue
                beats = (ag[k] > ag[j]) | (ag[k] == ag[j]) if k < j else (ag[k] > ag[j])
                c = c + jnp.where(beats, one, zero)
            cnt.append(c)
        top = zero
        sec = zero
        for j in range(4):
            top = top + jnp.where(cnt[j] == 0.0, ag[j], zero)
            sec = sec + jnp.where(cnt[j] == 1.0, ag[j], zero)
        gscore.append(top + sec)
        chosen.append([jnp.where(cnt[j] < 2.0, one, zero) for j in range(4)])
    best = gscore[0]
    gi = zero
    for g in range(1, N_EXPERT_GROUPS):
        better = gscore[g] > best
        gi = jnp.where(better, float(g), gi)
        best = jnp.where(better, gscore[g], best)
    m = []
    sg = []
    for j in range(4):
        mj = chosen[0][j]
        sj = sv[j]
        for g in range(1, N_EXPERT_GROUPS):
            isg = gi == float(g)
            mj = jnp.where(isg, chosen[g][j], mj)
            sj = jnp.where(isg, sv[4 * g + j], sj)
        m.append(mj)
        sg.append(sj)
    pidx = zero
    s_lo = zero
    s_hi = zero
    for idx, (p0, p1) in enumerate(PAIRS):
        hit = (m[p0] * m[p1]) > 0.5
        pidx = jnp.where(hit, float(idx), pidx)
        s_lo = jnp.where(hit, sg[p0], s_lo)
        s_hi = jnp.where(hit, sg[p1], s_hi)
    den = s_lo + s_hi
    cls = (gi * float(len(PAIRS)) + pidx).astype(jnp.int32)
    return s_lo / den, s_hi / den, cls


def _tail(x, mix, p, lng, lnb, wrt_ref, br_ref, r_ref, cls_ref):
    x1 = _ln(ALPHA * x + mix, lng, lnb)
    g_lo, g_hi, cls = _route(x1, wrt_ref, br_ref)
    n = x.shape[0]
    row = lax.broadcasted_iota(jnp.int32, (LANES, n), 0)
    meta_t = jnp.where(row == 0, g_lo, jnp.where(row == 1, g_hi, 0.0))
    r_ref[:, 0:D_MODEL] = x1
    r_ref[:, D_MODEL:D_MODEL + D_PLE] = p
    r_ref[:, D_MODEL + D_PLE:ROW_W] = meta_t.T
    cls_ref[...] = cls


def _pool_kernel(x_ref, halo_ref, exts_ref, p_ref, wp_ref, sc_ref, lng_ref, lnb_ref, wrt_ref, br_ref,
                 r_ref, cls_ref, ext_scr, *, n_prompt_steps, steps_per_seq):
    s = pl.program_id(0)

    def finish(x, diff_parts, p):
        ys = [jnp.dot(diff_parts[g].astype(BF16), wp_ref[g], preferred_element_type=F32)
              for g in range(len(POOL_WINDOWS))]
        mix = jnp.concatenate(ys, axis=1) * sc_ref[...]
        _tail(x, mix, p, lng_ref[...], lnb_ref[...], wrt_ref, br_ref, r_ref, cls_ref)

    @pl.when(s < n_prompt_steps)
    def _():
        t = s % steps_per_seq
        keep = jnp.where(t > 0, 1.0, 0.0)
        ext_scr[0:HALO, :] = halo_ref[...] * keep
        ext_scr[HALO:HALO + TT, :] = x_ref[...]
        tpos = (t * TT + lax.broadcasted_iota(jnp.int32, (TT, 1), 0)).astype(F32)
        parts = []
        for g, w in enumerate(POOL_WINDOWS):
            cols = pl.ds(g * POOL_GROUP, POOL_GROUP)
            xg = ext_scr[pl.ds(HALO, TT), cols]
            acc = xg
            for k in range(1, w):
                acc = acc + ext_scr[pl.ds(HALO - k, TT), cols]
            cnt = jnp.minimum(float(w), tpos + 1.0)
            parts.append(acc / cnt - xg)
        finish(x_ref[...], parts, p_ref[...])

    @pl.when(s >= n_prompt_steps)
    def _():
        nb = exts_ref.shape[0]
        hist = exts_ref.shape[1] - HALO
        parts = []
        for g, w in enumerate(POOL_WINDOWS):
            cols = pl.ds(g * POOL_GROUP, POOL_GROUP)
            xg = exts_ref[:, pl.ds(HALO, hist), cols]
            acc = xg
            for k in range(1, w):
                acc = acc + exts_ref[:, pl.ds(HALO - k, hist), cols]
            parts.append((acc / float(w) - xg).reshape(nb * hist, POOL_GROUP))
        x = exts_ref[:, pl.ds(HALO, hist), :].reshape(nb * hist, D_MODEL)
        finish(x, parts, p_ref[...])


def _pool_layer(xp_flat, ext_s, p_flat, wp, sc, lng, lnb, wrt, br, *, n_prompt, seq_len):
    ntok = p_flat.shape[0]
    n_steps = ntok // TT
    npst = n_prompt // TT
    sps = seq_len // TT
    dec_seq = ext_s.shape[1] - HALO
    nb = TT // dec_seq
    kern = functools.partial(_pool_kernel, n_prompt_steps=npst, steps_per_seq=sps)
    const = lambda s: (0, 0)
    return pl.pallas_call(
        kern,
        grid=(n_steps,),
        in_specs=[
            pl.BlockSpec((TT, D_MODEL), lambda s: (jnp.minimum(s, npst - 1), 0)),
            pl.BlockSpec((HALO, D_MODEL),
                         lambda s: (jnp.maximum(jnp.minimum(s, npst - 1) * (TT // HALO) - 1, 0), 0)),
            pl.BlockSpec((nb, HALO + dec_seq, D_MODEL), lambda s: (jnp.maximum(s - npst, 0), 0, 0)),
            pl.BlockSpec((TT, D_PLE), lambda s: (s, 0)),
            pl.BlockSpec((len(POOL_WINDOWS), POOL_GROUP, POOL_GROUP), lambda s: (0, 0, 0)),
            pl.BlockSpec((1, D_MODEL), const),
            pl.BlockSpec((1, D_MODEL), const),
            pl.BlockSpec((1, D_MODEL), const),
            pl.BlockSpec((N_EXPERTS, D_MODEL), const),
            pl.BlockSpec((N_EXPERTS, 1), const),
        ],
        out_specs=[
            pl.BlockSpec((TT, ROW_W), lambda s: (s, 0)),
            pl.BlockSpec((1, TT), lambda s: (0, s)),
        ],
        out_shape=[
            jax.ShapeDtypeStruct((ntok, ROW_W), F32),
            jax.ShapeDtypeStruct((1, ntok), jnp.int32),
        ],
        scratch_shapes=[pltpu.VMEM((HALO + TT, D_MODEL), F32)],
        compiler_params=pltpu.CompilerParams(dimension_semantics=("arbitrary",),
                                             vmem_limit_bytes=VMEM_LIMIT),
        name="pool_mixer",
    )(xp_flat, xp_flat, ext_s, p_flat, wp, sc, lng, lnb, wrt, br)


def _moe_kernel(tok_ref, elo_ref, ehi_ref, nrows_ref, r_hbm, wg_lo, wu_lo, wd_lo, wg_hi, wu_hi, wd_hi,
                wpg_ref, wpp_ref, lng_ref, lnb_ref, x_hbm, rbuf, ybuf, gsem, ssem):
    k = pl.program_id(0)
    n_tiles = pl.num_programs(0)
    slot = k % 2

    def gather_start(tile, s):
        def body(j, c):
            r = tok_ref[tile * TM + j]
            pltpu.make_async_copy(r_hbm.at[pl.ds(r, 1)], rbuf.at[s, pl.ds(j, 1)], gsem.at[s]).start()
            return c
        lax.fori_loop(0, TM, body, 0)

    def gather_wait(s):
        pltpu.make_async_copy(r_hbm.at[pl.ds(0, TM)], rbuf.at[s], gsem.at[s]).wait()

    def scatter_start(tile, s):
        def body(j, c):
            d = tok_ref[tile * TM + j]
            pltpu.make_async_copy(ybuf.at[s, pl.ds(j, 1)], x_hbm.at[pl.ds(d, 1)], ssem.at[s]).start()
            return c
        lax.fori_loop(0, nrows_ref[tile], body, 0)

    def scatter_wait(tile, s):
        def body(j, c):
            pltpu.make_async_copy(ybuf.at[s, pl.ds(0, 1)], x_hbm.at[pl.ds(0, 1)], ssem.at[s]).wait()
            return c
        lax.fori_loop(0, nrows_ref[tile], body, 0)

    next_valid = (k + 1 < n_tiles) & (nrows_ref[jnp.minimum(k + 1, n_tiles - 1)] != 0)

    @pl.when(k == 0)
    def _():
        gather_start(0, 0)

    @pl.when(nrows_ref[k] != 0)
    def _():
        @pl.when(next_valid)
        def _():
            gather_start(k + 1, 1 - slot)

        gather_wait(slot)
        x1 = rbuf[slot, :, 0:D_MODEL]
        p = rbuf[slot, :, D_MODEL:D_MODEL + D_PLE]
        gates = rbuf[slot, :, D_MODEL + D_PLE:ROW_W]
        xb = x1.astype(BF16)

        def expert(wg, wu, wd):
            hg = jnp.dot(xb, wg[...], preferred_element_type=F32)
            hu = jnp.dot(xb, wu[...], preferred_element_type=F32)
            h = (hg * jax.nn.sigmoid(hg)) * hu
            return jnp.dot(h.astype(BF16), wd[...], preferred_element_type=F32)

        moe = gates[:, 0:1] * expert(wg_lo, wu_lo, wd_lo)
        moe = moe + gates[:, 1:2] * expert(wg_hi, wu_hi, wd_hi)
        x2 = _ln(ALPHA * x1 + moe, lng_ref[0:1, :], lnb_ref[0:1, :])
        gate = jax.nn.sigmoid(jnp.dot(x2.astype(BF16), wpg_ref[...], preferred_element_type=F32))
        proj = jnp.dot(p.astype(BF16), wpp_ref[...], preferred_element_type=F32)
        ybuf[slot] = _ln(ALPHA * x2 + gate * proj, lng_ref[1:2, :], lnb_ref[1:2, :])

        @pl.when(k > 0)
        def _():
            scatter_wait(k - 1, 1 - slot)

        scatter_start(k, slot)

        @pl.when(jnp.logical_not(next_valid))
        def _():
            scatter_wait(k, slot)


def _moe_layer(rows, slot_tok, elo, ehi, nrows, wg, wu, wd, wpg, wpp, lng2, lnb2, *, layer):
    ntok = rows.shape[0]
    n_tiles = nrows.shape[0]
    lo = lambda k, tok, elo, ehi, nrows: (layer, elo[k], 0, 0)
    hi = lambda k, tok, elo, ehi, nrows: (layer, ehi[k], 0, 0)
    const2 = lambda k, tok, elo, ehi, nrows: (0, 0)
    gu_spec = lambda im: pl.BlockSpec((None, None, D_MODEL, D_EXPERT), im)
    dn_spec = lambda im: pl.BlockSpec((None, None, D_EXPERT, D_MODEL), im)
    return pl.pallas_call(
        _moe_kernel,
        grid_spec=pltpu.PrefetchScalarGridSpec(
            num_scalar_prefetch=4,
            grid=(n_tiles,),
            in_specs=[
                pl.BlockSpec(memory_space=pl.ANY),
                gu_spec(lo), gu_spec(lo), dn_spec(lo),
                gu_spec(hi), gu_spec(hi), dn_spec(hi),
                pl.BlockSpec((D_MODEL, D_MODEL), const2),
                pl.BlockSpec((D_PLE, D_MODEL), const2),
                pl.BlockSpec((2, D_MODEL), const2),
                pl.BlockSpec((2, D_MODEL), const2),
            ],
            out_specs=pl.BlockSpec(memory_space=pl.ANY),
            scratch_shapes=[pltpu.VMEM((2, TM, ROW_W), F32), pltpu.VMEM((2, TM, D_MODEL), F32),
                            pltpu.SemaphoreType.DMA((2,)), pltpu.SemaphoreType.DMA((2,))],
        ),
        out_shape=jax.ShapeDtypeStruct((ntok, D_MODEL), F32),
        compiler_params=pltpu.CompilerParams(dimension_semantics=("arbitrary",),
                                             vmem_limit_bytes=VMEM_LIMIT),
        name="moe_ple",
    )(slot_tok, elo, ehi, nrows, rows, wg, wu, wd, wg, wu, wd, wpg, wpp, lng2, lnb2)


def _route_plan(cls):
    ntok = cls.shape[0]
    n_tiles = -(-(ntok + N_CLASSES * (TM - 1)) // TM)
    ns = n_tiles * TM
    onehot = (cls[:, None] == jnp.arange(N_CLASSES, dtype=jnp.int32)[None, :]).astype(jnp.int32)
    csum = jnp.cumsum(onehot, axis=0)
    rank = jnp.sum(csum * onehot, axis=1) - 1
    counts = csum[-1]
    padded = ((counts + TM - 1) // TM) * TM
    ends = jnp.cumsum(padded)
    starts = ends - padded
    pos = jnp.sum(onehot * starts[None, :], axis=1) + rank
    tok = jnp.arange(ntok, dtype=jnp.int32)
    src = jnp.zeros((ns,), jnp.int32).at[pos].set(tok)
    tile_start = jnp.arange(n_tiles, dtype=jnp.int32) * TM
    tile_cls = jnp.sum((tile_start[:, None] >= ends[None, :]).astype(jnp.int32), axis=1)
    last_cls = jnp.max(jnp.where(counts > 0, jnp.arange(N_CLASSES, dtype=jnp.int32), 0))
    tile_cls = jnp.minimum(tile_cls, last_cls)
    nrows = jnp.clip((starts + counts)[tile_cls] - tile_start, 0, TM)
    grp = tile_cls // len(PAIRS)
    pidx = tile_cls % len(PAIRS)
    p0 = jnp.asarray([p[0] for p in PAIRS], jnp.int32)[pidx]
    p1 = jnp.asarray([p[1] for p in PAIRS], jnp.int32)[pidx]
    return src, grp * EXPERTS_PER_GROUP + p0, grp * EXPERTS_PER_GROUP + p1, nrows


def _kv_kernel(x_ref, w_ref, kf_ref, vf_ref, kb_ref, vb_ref):
    kv = jnp.dot(x_ref[...].astype(BF16), w_ref[...], preferred_element_type=F32)
    k = kv[:, 0:D_MODEL]
    v = kv[:, D_MODEL:2 * D_MODEL]
    kf_ref[...] = k
    vf_ref[...] = v
    kb_ref[...] = k.astype(BF16)
    vb_ref[...] = v.astype(BF16)


def _kv_proj(x, w, ntok):
    blk = lambda s: (s, 0)
    return pl.pallas_call(
        _kv_kernel,
        grid=(ntok // TT,),
        in_specs=[pl.BlockSpec((TT, D_MODEL), blk), pl.BlockSpec((D_MODEL, 2 * D_MODEL), lambda s: (0, 0))],
        out_specs=[pl.BlockSpec((TT, D_MODEL), blk)] * 4,
        out_shape=[jax.ShapeDtypeStruct((ntok, D_MODEL), F32)] * 2
        + [jax.ShapeDtypeStruct((ntok, D_MODEL), BF16)] * 2,
        compiler_params=pltpu.CompilerParams(dimension_semantics=("arbitrary",),
                                             vmem_limit_bytes=VMEM_LIMIT),
        name="kv_proj",
    )(x, w)


def _q_kernel(x_ref, w_ref, q_ref):
    q = jnp.dot(x_ref[...].astype(BF16), w_ref[...], preferred_element_type=F32)
    q_ref[...] = (q * (HEAD_DIM ** -0.5)).astype(q_ref.dtype)


def _q_proj(x, w, ntok):
    n_out = w.shape[1]
    return pl.pallas_call(
        _q_kernel,
        grid=(ntok // TT, n_out // D_MODEL),
        in_specs=[pl.BlockSpec((TT, D_MODEL), lambda s, c: (s, 0)),
                  pl.BlockSpec((D_MODEL, D_MODEL), lambda s, c: (0, c))],
        out_specs=pl.BlockSpec((TT, D_MODEL), lambda s, c: (s, c)),
        out_shape=jax.ShapeDtypeStruct((ntok, n_out), BF16),
        compiler_params=pltpu.CompilerParams(dimension_semantics=("arbitrary", "arbitrary"),
                                             vmem_limit_bytes=VMEM_LIMIT),
        name="q_proj",
    )(x, w)


def _attn_prompt_kernel(q_ref, kp_ref, kc_ref, vp_ref, vc_ref, o_ref, lse_ref, *, group):
    i = pl.program_id(2)
    dil = DIL_PATTERNS[group][1]
    a = lax.broadcasted_iota(jnp.int32, (QB, 2 * QB), 0)
    j = lax.broadcasted_iota(jnp.int32, (QB, 2 * QB), 1)
    delta = QB + a - j
    ok = (delta >= 0) & (delta <= QB) & ((j >= QB) | (i > 0))
    dist = (delta * dil).astype(F32)
    lane = lax.broadcasted_iota(jnp.int32, (QB, LANES), 1)
    first_half = lane < HEAD_DIM
    lse_blk = jnp.zeros((QB, LANES), F32)
    for hp in range(N_HEADS // 2):
        cols = pl.ds(hp * LANES, LANES)
        q2 = q_ref[:, cols]
        k2 = jnp.concatenate([kp_ref[:, cols], kc_ref[:, cols]], axis=0)
        v2 = jnp.concatenate([vp_ref[:, cols], vc_ref[:, cols]], axis=0)
        outs = []
        for half in range(2):
            h = 2 * hp + half
            in_head = first_half if half == 0 else jnp.logical_not(first_half)
            qm = jnp.where(in_head, q2, jnp.zeros_like(q2))
            s = lax.dot_general(qm, k2, NT_DIMS, preferred_element_type=F32)
            s = jnp.where(ok, s - float(SLOPES[group, h]) * dist, NEG)
            mx = jnp.max(s, axis=1, keepdims=True)
            pexp = jnp.exp(s - mx)
            den = jnp.sum(pexp, axis=1, keepdims=True)
            pv = jnp.dot(pexp.astype(BF16), v2, preferred_element_type=F32)
            outs.append(pv / den)
            lse_blk = jnp.where(lane == h, mx + jnp.log(den), lse_blk)
        o_ref[:, cols] = jnp.where(first_half, outs[0], outs[1]).astype(o_ref.dtype)
    lse_ref[...] = lse_blk


def _attn_prompt(q, kb, vb, *, group, batch, seq_len):
    dil = DIL_PATTERNS[group][1]
    rows = seq_len // dil
    nq = rows // QB
    qv = q.reshape(batch, rows, dil * N_DIL * D_MODEL)
    kv = kb.reshape(batch, rows, dil * D_MODEL)
    vv = vb.reshape(batch, rows, dil * D_MODEL)
    cur = lambda b, r, i: (b, i, r)
    prev = lambda b, r, i: (b, jnp.maximum(i - 1, 0), r)
    blk = lambda im: pl.BlockSpec((None, QB, D_MODEL), im)
    o, lse = pl.pallas_call(
        functools.partial(_attn_prompt_kernel, group=group),
        grid=(batch, dil, nq),
        in_specs=[pl.BlockSpec((None, QB, D_MODEL), lambda b, r, i: (b, i, r * N_DIL + group)),
                  blk(prev), blk(cur), blk(prev), blk(cur)],
        out_specs=[blk(cur), pl.BlockSpec((None, QB, LANES), cur)],
        out_shape=[jax.ShapeDtypeStruct((batch, rows, dil * D_MODEL), BF16),
                   jax.ShapeDtypeStruct((batch, rows, dil * LANES), F32)],
        compiler_params=pltpu.CompilerParams(dimension_semantics=("arbitrary",) * 3,
                                             vmem_limit_bytes=VMEM_LIMIT),
        name=f"attn_prompt_g{group}",
    )(qv, kv, kv, vv, vv)
    return o.reshape(batch * seq_len, D_MODEL), lse.reshape(batch * seq_len, LANES)


def _attn_sample_kernel(q_ref, kt_ref, vt_ref, kn_ref, vn_ref, sl_ref, o_ref, *shift_refs, dec_seq):
    n_heads = q_ref.shape[0]
    nrow = N_DIL * dec_seq
    ncol = W_MAX + LANES
    new0 = LANES - dec_seq
    row = lax.broadcasted_iota(jnp.int32, (nrow, ncol), 0)
    col = lax.broadcasted_iota(jnp.int32, (nrow, ncol), 1)
    grp = row // dec_seq
    kpos = jnp.where(col < W_MAX, col, col - new0)
    dist = W_MAX + (row - grp * dec_seq) - kpos
    win = jnp.where(grp == 0, DIL_PATTERNS[0][0], jnp.where(grp == 1, DIL_PATTERNS[1][0], DIL_PATTERNS[2][0]))
    dmask = jnp.where(grp == 0, DIL_PATTERNS[0][1] - 1,
                      jnp.where(grp == 1, DIL_PATTERNS[1][1] - 1, DIL_PATTERNS[2][1] - 1))
    ok = ((col < W_MAX) | (col >= W_MAX + new0)) & (dist >= 0) & (dist <= win) & ((dist & dmask) == 0)
    distf = dist.astype(F32)
    lane = lax.broadcasted_iota(jnp.int32, (dec_seq, LANES), 1)

    def scores(h):
        q = q_ref[h].astype(BF16)
        s = jnp.concatenate([jnp.dot(q, kt_ref[h].astype(BF16), preferred_element_type=F32),
                             jnp.dot(q, kn_ref[h].astype(BF16), preferred_element_type=F32)], axis=1)
        s = jnp.where(ok, s - sl_ref[h][:, 0:1] * distf, NEG)
        mx = jnp.max(s, axis=1, keepdims=True)
        pexp = jnp.exp(s - mx)
        den = jnp.sum(pexp, axis=1, keepdims=True)
        return pexp.astype(BF16), den, mx + jnp.log(den)

    def mix_groups(o, lse):
        parts = [o[g * dec_seq:(g + 1) * dec_seq] for g in range(N_DIL)]
        ls = [lse[g * dec_seq:(g + 1) * dec_seq] for g in range(N_DIL)]
        top = jnp.maximum(jnp.maximum(ls[0], ls[1]), ls[2])
        ws = [jnp.exp(l - top) for l in ls]
        return (ws[0] * parts[0] + ws[1] * parts[1] + ws[2] * parts[2]) / (ws[0] + ws[1] + ws[2])

    for hp in range(n_heads // 2):
        vt2 = vt_ref[2 * hp:2 * hp + 2].reshape(2 * HEAD_DIM, W_MAX).astype(BF16)
        vn2 = vn_ref[2 * hp:2 * hp + 2].reshape(2 * HEAD_DIM, LANES).astype(BF16)
        mixed = []
        for half in range(2):
            pb, den, lse = scores(2 * hp + half)
            acc = (lax.dot_general(pb[:, 0:W_MAX], vt2, NT_DIMS, preferred_element_type=F32)
                   + lax.dot_general(pb[:, W_MAX:ncol], vn2, NT_DIMS, preferred_element_type=F32))
            mixed.append(mix_groups(acc / den, lse))
        o_ref[:, hp * LANES:(hp + 1) * LANES] = jnp.where(lane < HEAD_DIM, mixed[0], mixed[1])

    if shift_refs:
        lane_s = lax.broadcasted_iota(jnp.int32, (HEAD_DIM, LANES), 1)
        n_chunks = W_MAX // LANES
        for src_ref, new_ref, dst_ref in ((kt_ref, kn_ref, shift_refs[0]), (vt_ref, vn_ref, shift_refs[1])):
            for h in range(n_heads):
                rolled = [pltpu.roll(src_ref[h, :, c * LANES:(c + 1) * LANES], new0, 1) for c in range(n_chunks)]
                rolled.append(new_ref[h])
                for c in range(n_chunks):
                    dst_ref[h, :, c * LANES:(c + 1) * LANES] = jnp.where(lane_s < new0, rolled[c], rolled[c + 1])


def _attn_sample(q4, kt, vt, knp, vnp, slope4, *, dec_seq, heads_per_step, shift):
    nb = q4.shape[0]
    hps = heads_per_step
    nrow = N_DIL * dec_seq
    per_head = lambda last2: pl.BlockSpec((None, hps) + last2, lambda b, c: (b, c, 0, 0))
    out_specs = [pl.BlockSpec((None, dec_seq, hps * HEAD_DIM), lambda b, c: (b, 0, c))]
    out_shape = [jax.ShapeDtypeStruct((nb, dec_seq, D_MODEL), F32)]
    if shift:
        out_specs += [per_head((HEAD_DIM, W_MAX))] * 2
        out_shape += [jax.ShapeDtypeStruct(kt.shape, kt.dtype)] * 2
    return pl.pallas_call(
        functools.partial(_attn_sample_kernel, dec_seq=dec_seq),
        grid=(nb, N_HEADS // hps),
        in_specs=[per_head((nrow, HEAD_DIM)), per_head((HEAD_DIM, W_MAX)), per_head((HEAD_DIM, W_MAX)),
                  per_head((HEAD_DIM, LANES)), per_head((HEAD_DIM, LANES)),
                  pl.BlockSpec((hps, nrow, LANES), lambda b, c: (c, 0, 0))],
        out_specs=out_specs,
        out_shape=out_shape,
        compiler_params=pltpu.CompilerParams(dimension_semantics=("arbitrary", "arbitrary"),
                                             vmem_limit_bytes=VMEM_LIMIT),
        name="attn_sample_shift" if shift else "attn_sample",
    )(q4, kt, vt, knp, vnp, slope4)


def _attn_out_kernel(x_ref, o0_ref, o1_ref, o2_ref, l0_ref, l1_ref, l2_ref, os_ref, e_ref, wo_ref, p_ref,
                     lng_ref, lnb_ref, wrt_ref, br_ref, r_ref, cls_ref, *, n_prompt_steps):
    s = pl.program_id(0)

    def finish(o):
        mix = jnp.dot(o.astype(BF16), wo_ref[...], preferred_element_type=F32)
        _tail(x_ref[...], mix, p_ref[...], lng_ref[...], lnb_ref[...], wrt_ref, br_ref, r_ref, cls_ref)

    @pl.when(s < n_prompt_steps)
    def _():
        lses = [l0_ref[...], l1_ref[...], l2_ref[...]]
        top = jnp.maximum(jnp.maximum(lses[0], lses[1]), lses[2])
        ws = [jnp.exp(l - top) for l in lses]
        tot = ws[0] + ws[1] + ws[2]
        o = jnp.zeros((TT, D_MODEL), F32)
        for wgt, o_ref in zip(ws, (o0_ref, o1_ref, o2_ref)):
            hi, lo = _split_bf16(wgt / tot)
            wide = (jnp.dot(hi, e_ref[...], preferred_element_type=F32)
                    + jnp.dot(lo, e_ref[...], preferred_element_type=F32))
            o = o + wide * o_ref[...].astype(F32)
        finish(o)

    @pl.when(s >= n_prompt_steps)
    def _():
        finish(os_ref[...])


def _attn_out_layer(x, o3, l3, o_s, expand, wo, p_flat, lng, lnb, wrt, br, *, n_prompt):
    ntok = p_flat.shape[0]
    npst = n_prompt // TT
    blk = lambda s: (s, 0)
    pblk = lambda s: (jnp.minimum(s, npst - 1), 0)
    sblk = lambda s: (jnp.maximum(s - npst, 0), 0)
    const = lambda s: (0, 0)
    return pl.pallas_call(
        functools.partial(_attn_out_kernel, n_prompt_steps=npst),
        grid=(ntok // TT,),
        in_specs=[pl.BlockSpec((TT, D_MODEL), blk)]
        + [pl.BlockSpec((TT, D_MODEL), pblk)] * 3
        + [pl.BlockSpec((TT, LANES), pblk)] * 3
        + [pl.BlockSpec((TT, D_MODEL), sblk),
           pl.BlockSpec((LANES, D_MODEL), const),
           pl.BlockSpec((D_MODEL, D_MODEL), const),
           pl.BlockSpec((TT, D_PLE), blk),
           pl.BlockSpec((1, D_MODEL), const),
           pl.BlockSpec((1, D_MODEL), const),
           pl.BlockSpec((N_EXPERTS, D_MODEL), const),
           pl.BlockSpec((N_EXPERTS, 1), const)],
        out_specs=[pl.BlockSpec((TT, ROW_W), blk), pl.BlockSpec((1, TT), lambda s: (0, s))],
        out_shape=[jax.ShapeDtypeStruct((ntok, ROW_W), F32), jax.ShapeDtypeStruct((1, ntok), jnp.int32)],
        compiler_params=pltpu.CompilerParams(dimension_semantics=("arbitrary",),
                                             vmem_limit_bytes=VMEM_LIMIT),
        name="attn_out",
    )(x, *o3, *l3, o_s, expand, wo, p_flat, lng, lnb, wrt, br)


def kernel(x_prompt, x_sample, state_pool, cache_k, cache_v, p_prompt, p_sample, w_pool, pool_scale, w_kv,
           w_q, w_o, ln_g, ln_b, w_router, b_router, w_exp_gate, w_exp_up, w_exp_down, w_ple_gate,
           w_ple_proj):
    batch, seq_len, d = x_prompt.shape
    nb, dec_seq, _ = x_sample.shape
    wbuf = cache_k.shape[1]
    n_prompt = batch * seq_len
    n_sample = nb * dec_seq
    ntok = n_prompt + n_sample
    assert d == D_MODEL and wbuf == W_MAX and dec_seq == 8
    assert seq_len % (QB * DIL_PATTERNS[2][1]) == 0 and n_sample % TT == 0 and seq_len % TT == 0

    wrt2 = w_router.T.astype(BF16)
    br = b_router.astype(F32).reshape(N_EXPERTS, 1)
    wg, wu, wd = w_exp_gate.astype(BF16), w_exp_up.astype(BF16), w_exp_down.astype(BF16)
    wpg, wpp = w_ple_gate.astype(BF16), w_ple_proj.astype(BF16)
    wp = w_pool.astype(BF16)
    head_of_col = np.arange(D_MODEL) // HEAD_DIM
    expand = jnp.asarray(np.arange(LANES)[:, None] == head_of_col[None, :], BF16)
    slope4 = jnp.asarray(np.ascontiguousarray(np.broadcast_to(
        np.repeat(SLOPES.T, dec_seq, axis=1)[:, :, None], (N_HEADS, N_DIL * dec_seq, LANES))))

    p_flat = jnp.concatenate([p_prompt.reshape(DEPTH, n_prompt, D_PLE),
                              p_sample.reshape(DEPTH, n_sample, D_PLE)], axis=1)

    def moe_stage(rows, cls, layer):
        slot_tok, elo, ehi, nrows = _route_plan(cls.reshape(ntok))
        return _moe_layer(rows, slot_tok, elo, ehi, nrows, wg, wu, wd, wpg[layer], wpp[layer],
                          ln_g[layer, 1:3], ln_b[layer, 1:3], layer=layer)

    def to_heads_minor(a, width):
        a = a.reshape(nb, dec_seq, N_HEADS, HEAD_DIM).transpose(0, 2, 3, 1)
        return jnp.pad(a, ((0, 0), (0, 0), (0, 0), (width - dec_seq, 0)))

    x = x_prompt.reshape(n_prompt, D_MODEL)
    xs = x_sample
    pool_p, pool_s = [], []
    for i in range(N_A):
        ext_s = jnp.concatenate([jnp.zeros((nb, HALO - POOL_HIST, D_MODEL), F32), state_pool[i], xs], axis=1)
        pool_p.append(x[:n_prompt].reshape(batch, seq_len, D_MODEL)[:, seq_len - POOL_HIST:])
        pool_s.append(ext_s[:, HALO + dec_seq - POOL_HIST:])
        rows, cls = _pool_layer(x, ext_s, p_flat[i], wp[i], pool_scale[i].reshape(1, D_MODEL),
                                ln_g[i, 0].reshape(1, D_MODEL), ln_b[i, 0].reshape(1, D_MODEL), wrt2, br,
                                n_prompt=n_prompt, seq_len=seq_len)
        x = moe_stage(rows, cls, i)
        xs = x[n_prompt:ntok].reshape(nb, dec_seq, D_MODEL)

    kf, vf, kb, vb = _kv_proj(x, w_kv.astype(BF16), ntok)
    knp = to_heads_minor(kf[n_prompt:], LANES)
    vnp = to_heads_minor(vf[n_prompt:], LANES)
    kt = cache_k.transpose(0, 2, 3, 1)
    vt = cache_v.transpose(0, 2, 3, 1)

    for jl in range(N_B):
        i = N_A + jl
        q = _q_proj(x, w_q[jl].astype(BF16), ntok)
        o3, l3 = [], []
        for g in range(N_DIL):
            o_g, l_g = _attn_prompt(q[:n_prompt], kb[:n_prompt], vb[:n_prompt], group=g, batch=batch,
                                    seq_len=seq_len)
            o3.append(o_g)
            l3.append(l_g)
        q4 = (q[n_prompt:].astype(F32).reshape(nb, dec_seq, N_DIL, N_HEADS, HEAD_DIM)
              .transpose(0, 3, 2, 1, 4).reshape(nb, N_HEADS, N_DIL * dec_seq, HEAD_DIM))
        if jl == 0:
            o_s, kt_new, vt_new = _attn_sample(q4, kt, vt, knp, vnp, slope4, dec_seq=dec_seq,
                                               heads_per_step=4, shift=True)
        else:
            (o_s,) = _attn_sample(q4, kt, vt, knp, vnp, slope4, dec_seq=dec_seq, heads_per_step=8, shift=False)
        rows, cls = _attn_out_layer(x, o3, l3, o_s.reshape(n_sample, D_MODEL), expand, w_o[jl].astype(BF16),
                                    p_flat[i], ln_g[i, 0].reshape(1, D_MODEL), ln_b[i, 0].reshape(1, D_MODEL),
                                    wrt2, br, n_prompt=n_prompt)
        x = moe_stage(rows, cls, i)

    keep_p = max(seq_len - W_MAX, 0)
    hd = (N_HEADS, HEAD_DIM)
    k_prompt = kf[:n_prompt].reshape(batch, seq_len, *hd)[:, keep_p:]
    v_prompt = vf[:n_prompt].reshape(batch, seq_len, *hd)[:, keep_p:]
    return (x[:n_prompt].reshape(batch, seq_len, D_MODEL),
            x[n_prompt:ntok].reshape(nb, dec_seq, D_MODEL),
            jnp.stack(pool_p, 0), jnp.stack(pool_s, 0),
            k_prompt, v_prompt,
            kt_new.transpose(0, 3, 1, 2), vt_new.transpose(0, 3, 1, 2))
```

```python
import functools

import numpy as np
import jax
import jax.numpy as jnp
from jax import lax
from jax.experimental import pallas as pl
from jax.experimental.pallas import tpu as pltpu

D_MODEL = 1024
DEPTH = 4
N_A = DEPTH // 2
N_B = DEPTH - N_A
POOL_WINDOWS = (2, 4, 8, 16)
POOL_GROUP = D_MODEL // len(POOL_WINDOWS)
POOL_HIST = max(POOL_WINDOWS) - 1
HEAD_DIM = 64
N_HEADS = D_MODEL // HEAD_DIM
DIL_PATTERNS = ((128, 1), (512, 4), (2048, 16))
N_DIL = len(DIL_PATTERNS)
W_MAX = 2048
N_EXPERTS = 16
N_EXPERT_GROUPS = 4
EXPERTS_PER_GROUP = 4
D_EXPERT = 512
D_PLE = 256
ALPHA = (2 * DEPTH) ** 0.25
LN_EPS = 1e-5

PAIRS = ((0, 1), (0, 2), (0, 3), (1, 2), (1, 3), (2, 3))
N_CLASSES = N_EXPERT_GROUPS * len(PAIRS)

LANES = 128
HALO = 16
TT = 512
TM = 256
DMA_UNROLL = 8
assert D_MODEL == 8 * LANES
QB = 128
ROW_W = D_MODEL + D_PLE + LANES
VMEM_LIMIT = 48 * 1024 * 1024
NEG = -1e30

F32 = jnp.float32
BF16 = jnp.bfloat16
NT_DIMS = (((1,), (1,)), ((), ()))


def _alibi_slopes():
    n = N_DIL * N_HEADS
    return (2.0 ** (-8.0 * np.arange(1, n + 1) / n)).astype(np.float32).reshape(N_DIL, N_HEADS)


SLOPES = _alibi_slopes()


def _ln(x, g, b):
    mu = jnp.mean(x, axis=-1, keepdims=True)
    xc = x - mu
    var = jnp.mean(xc * xc, axis=-1, keepdims=True)
    return xc * lax.rsqrt(var + LN_EPS) * g + b


def _split_bf16(x):
    hi = x.astype(BF16)
    lo = (x - hi.astype(F32)).astype(BF16)
    return hi, lo


def _route(x1, wrt_ref, br_ref):
    z = lax.dot_general(wrt_ref[...], x1.astype(BF16), NT_DIMS, preferred_element_type=F32)
    s = jax.nn.sigmoid(z)
    sel = s + br_ref[...]
    a = [sel[k:k + 1, :] for k in range(N_EXPERTS)]
    sv = [s[k:k + 1, :] for k in range(N_EXPERTS)]
    one = jnp.ones_like(a[0])
    zero = jnp.zeros_like(a[0])
    gscore, chosen = [], []
    for g in range(N_EXPERT_GROUPS):
        ag = a[4 * g:4 * g + 4]
        cnt = []
        for j in range(4):
            c = zero
            for k in range(4):
                if k == j:
                    continue
                beats = (ag[k] > ag[j]) | (ag[k] == ag[j]) if k < j else (ag[k] > ag[j])
                c = c + jnp.where(beats, one, zero)
            cnt.append(c)
        top = zero
        sec = zero
        for j in range(4):
            top = top + jnp.where(cnt[j] == 0.0, ag[j], zero)
            sec = sec + jnp.where(cnt[j] == 1.0, ag[j], zero)
        gscore.append(top + sec)
        chosen.append([jnp.where(cnt[j] < 2.0, one, zero) for j in range(4)])
    best = gscore[0]
    gi = zero
    for g in range(1, N_EXPERT_GROUPS):
        better = gscore[g] > best
        gi = jnp.where(better, float(g), gi)
        best = jnp.where(better, gscore[g], best)
    m = []
    sg = []
    for j in range(4):
        mj = chosen[0][j]
        sj = sv[j]
        for g in range(1, N_EXPERT_GROUPS):
            isg = gi == float(g)
            mj = jnp.where(isg, chosen[g][j], mj)
            sj = jnp.where(isg, sv[4 * g + j], sj)
        m.append(mj)
        sg.append(sj)
    pidx = zero
    s_lo = zero
    s_hi = zero
    for idx, (p0, p1) in enumerate(PAIRS):
        hit = (m[p0] * m[p1]) > 0.5
        pidx = jnp.where(hit, float(idx), pidx)
        s_lo = jnp.where(hit, sg[p0], s_lo)
        s_hi = jnp.where(hit, sg[p1], s_hi)
    den = s_lo + s_hi
    cls = (gi * float(len(PAIRS)) + pidx).astype(jnp.int32)
    return s_lo / den, s_hi / den, cls


def _tail(x, mix, p, lng, lnb, wrt_ref, br_ref, r_ref, cls_ref):
    x1 = _ln(ALPHA * x + mix, lng, lnb)
    g_lo, g_hi, cls = _route(x1, wrt_ref, br_ref)
    n = x.shape[0]
    row = lax.broadcasted_iota(jnp.int32, (LANES, n), 0)
    meta_t = jnp.where(row == 0, g_lo, jnp.where(row == 1, g_hi, 0.0))
    r_ref[:, 0:D_MODEL] = x1
    r_ref[:, D_MODEL:D_MODEL + D_PLE] = p
    r_ref[:, D_MODEL + D_PLE:ROW_W] = meta_t.T
    cls_ref[...] = cls


def _pool_kernel(x_ref, halo_ref, exts_ref, p_ref, wp_ref, sc_ref, lng_ref, lnb_ref, wrt_ref, br_ref,
                 r_ref, cls_ref, ext_scr, *, n_prompt_steps, steps_per_seq):
    s = pl.program_id(0)

    def finish(x, diff_parts, p):
        ys = [jnp.dot(diff_parts[g].astype(BF16), wp_ref[g], preferred_element_type=F32)
              for g in range(len(POOL_WINDOWS))]
        mix = jnp.concatenate(ys, axis=1) * sc_ref[...]
        _tail(x, mix, p, lng_ref[...], lnb_ref[...], wrt_ref, br_ref, r_ref, cls_ref)

    @pl.when(s < n_prompt_steps)
    def _():
        t = s % steps_per_seq
        keep = jnp.where(t > 0, 1.0, 0.0)
        ext_scr[0:HALO, :] = halo_ref[...] * keep
        ext_scr[HALO:HALO + TT, :] = x_ref[...]
        tpos = (t * TT + lax.broadcasted_iota(jnp.int32, (TT, 1), 0)).astype(F32)
        parts = []
        for g, w in enumerate(POOL_WINDOWS):
            cols = pl.ds(g * POOL_GROUP, POOL_GROUP)
            xg = ext_scr[pl.ds(HALO, TT), cols]
            acc = xg
            for k in range(1, w):
                acc = acc + ext_scr[pl.ds(HALO - k, TT), cols]
            cnt = jnp.minimum(float(w), tpos + 1.0)
            parts.append(acc / cnt - xg)
        finish(x_ref[...], parts, p_ref[...])

    @pl.when(s >= n_prompt_steps)
    def _():
        nb = exts_ref.shape[0]
        hist = exts_ref.shape[1] - HALO
        parts = []
        for g, w in enumerate(POOL_WINDOWS):
            cols = pl.ds(g * POOL_GROUP, POOL_GROUP)
            xg = exts_ref[:, pl.ds(HALO, hist), cols]
            acc = xg
            for k in range(1, w):
                acc = acc + exts_ref[:, pl.ds(HALO - k, hist), cols]
            parts.append((acc / float(w) - xg).reshape(nb * hist, POOL_GROUP))
        x = exts_ref[:, pl.ds(HALO, hist), :].reshape(nb * hist, D_MODEL)
        finish(x, parts, p_ref[...])


def _pool_layer(xp_flat, ext_s, p_flat, wp, sc, lng, lnb, wrt, br, *, n_prompt, seq_len):
    ntok = p_flat.shape[0]
    n_steps = ntok // TT
    npst = n_prompt // TT
    sps = seq_len // TT
    dec_seq = ext_s.shape[1] - HALO
    nb = TT // dec_seq
    kern = functools.partial(_pool_kernel, n_prompt_steps=npst, steps_per_seq=sps)
    const = lambda s: (0, 0)
    return pl.pallas_call(
        kern,
        grid=(n_steps,),
        in_specs=[
            pl.BlockSpec((TT, D_MODEL), lambda s: (jnp.minimum(s, npst - 1), 0)),
            pl.BlockSpec((HALO, D_MODEL),
                         lambda s: (jnp.maximum(jnp.minimum(s, npst - 1) * (TT // HALO) - 1, 0), 0)),
            pl.BlockSpec((nb, HALO + dec_seq, D_MODEL), lambda s: (jnp.maximum(s - npst, 0), 0, 0)),
            pl.BlockSpec((TT, D_PLE), lambda s: (s, 0)),
            pl.BlockSpec((len(POOL_WINDOWS), POOL_GROUP, POOL_GROUP), lambda s: (0, 0, 0)),
            pl.BlockSpec((1, D_MODEL), const),
            pl.BlockSpec((1, D_MODEL), const),
            pl.BlockSpec((1, D_MODEL), const),
            pl.BlockSpec((N_EXPERTS, D_MODEL), const),
            pl.BlockSpec((N_EXPERTS, 1), const),
        ],
        out_specs=[
            pl.BlockSpec((TT, ROW_W), lambda s: (s, 0)),
            pl.BlockSpec((1, TT), lambda s: (0, s)),
        ],
        out_shape=[
            jax.ShapeDtypeStruct((ntok, ROW_W), F32),
            jax.ShapeDtypeStruct((1, ntok), jnp.int32),
        ],
        scratch_shapes=[pltpu.VMEM((HALO + TT, D_MODEL), F32)],
        compiler_params=pltpu.CompilerParams(dimension_semantics=("arbitrary",),
                                             vmem_limit_bytes=VMEM_LIMIT),
        name="pool_mixer",
    )(xp_flat, xp_flat, ext_s, p_flat, wp, sc, lng, lnb, wrt, br)


def _moe_kernel(tok_ref, elo_ref, ehi_ref, nrows_ref, r_hbm, wg_lo, wu_lo, wd_lo, wg_hi, wu_hi, wd_hi,
                wpg_ref, wpp_ref, lng_ref, lnb_ref, x_hbm, rbuf, ybuf, gsem, ssem):
    k = pl.program_id(0)
    n_tiles = pl.num_programs(0)
    slot = k % 2

    def gather_start(tile, s):
        def body(j, c):
            r = tok_ref[tile * TM + j]
            pltpu.make_async_copy(r_hbm.at[pl.ds(r, 1)], rbuf.at[s, pl.ds(j, 1)], gsem.at[s]).start()
            return c
        lax.fori_loop(0, TM, body, 0, unroll=DMA_UNROLL)

    def gather_wait(s):
        pltpu.make_async_copy(r_hbm.at[pl.ds(0, TM)], rbuf.at[s], gsem.at[s]).wait()

    def scatter_row(tile, s, j):
        d = tok_ref[tile * TM + j]
        pltpu.make_async_copy(ybuf.at[s, pl.ds(j, 1)], x_hbm.at[pl.ds(d, 1)], ssem.at[s]).start()

    def scatter_start(tile, s):
        n = nrows_ref[tile]

        def group(g, c):
            for u in range(DMA_UNROLL):
                scatter_row(tile, s, g * DMA_UNROLL + u)
            return c

        def single(j, c):
            scatter_row(tile, s, j)
            return c

        lax.fori_loop(0, n // DMA_UNROLL, group, 0)
        lax.fori_loop((n // DMA_UNROLL) * DMA_UNROLL, n, single, 0)

    def scatter_wait(tile, s):
        n = nrows_ref[tile]
        for bit in [1 << b for b in range(TM.bit_length())]:
            if bit >= 8:
                desc = pltpu.make_async_copy(ybuf.at[s, pl.ds(0, bit)], x_hbm.at[pl.ds(0, bit)], ssem.at[s])
            else:
                desc = pltpu.make_async_copy(ybuf.at[s, pl.ds(0, 8), pl.ds(0, bit * LANES)],
                                             x_hbm.at[pl.ds(0, 8), pl.ds(0, bit * LANES)], ssem.at[s])

            @pl.when((n & bit) != 0)
            def _():
                desc.wait()

    next_valid = (k + 1 < n_tiles) & (nrows_ref[jnp.minimum(k + 1, n_tiles - 1)] != 0)

    @pl.when(k == 0)
    def _():
        gather_start(0, 0)

    @pl.when(nrows_ref[k] != 0)
    def _():
        @pl.when(next_valid)
        def _():
            gather_start(k + 1, 1 - slot)

        gather_wait(slot)
        x1 = rbuf[slot, :, 0:D_MODEL]
        p = rbuf[slot, :, D_MODEL:D_MODEL + D_PLE]
        gates = rbuf[slot, :, D_MODEL + D_PLE:ROW_W]
        xb = x1.astype(BF16)

        def expert(wg, wu, wd):
            hg = jnp.dot(xb, wg[...], preferred_element_type=F32)
            hu = jnp.dot(xb, wu[...], preferred_element_type=F32)
            h = (hg * jax.nn.sigmoid(hg)) * hu
            return jnp.dot(h.astype(BF16), wd[...], preferred_element_type=F32)

        moe = gates[:, 0:1] * expert(wg_lo, wu_lo, wd_lo)
        moe = moe + gates[:, 1:2] * expert(wg_hi, wu_hi, wd_hi)
        x2 = _ln(ALPHA * x1 + moe, lng_ref[0:1, :], lnb_ref[0:1, :])
        gate = jax.nn.sigmoid(jnp.dot(x2.astype(BF16), wpg_ref[...], preferred_element_type=F32))
        proj = jnp.dot(p.astype(BF16), wpp_ref[...], preferred_element_type=F32)
        ybuf[slot] = _ln(ALPHA * x2 + gate * proj, lng_ref[1:2, :], lnb_ref[1:2, :])

        @pl.when(k > 0)
        def _():
            scatter_wait(k - 1, 1 - slot)

        scatter_start(k, slot)

        @pl.when(jnp.logical_not(next_valid))
        def _():
            scatter_wait(k, slot)


def _moe_layer(rows, slot_tok, elo, ehi, nrows, wg, wu, wd, wpg, wpp, lng2, lnb2, *, layer):
    ntok = rows.shape[0]
    n_tiles = nrows.shape[0]
    lo = lambda k, tok, elo, ehi, nrows: (layer, elo[k], 0, 0)
    hi = lambda k, tok, elo, ehi, nrows: (layer, ehi[k], 0, 0)
    const2 = lambda k, tok, elo, ehi, nrows: (0, 0)
    gu_spec = lambda im: pl.BlockSpec((None, None, D_MODEL, D_EXPERT), im)
    dn_spec = lambda im: pl.BlockSpec((None, None, D_EXPERT, D_MODEL), im)
    return pl.pallas_call(
        _moe_kernel,
        grid_spec=pltpu.PrefetchScalarGridSpec(
            num_scalar_prefetch=4,
            grid=(n_tiles,),
            in_specs=[
                pl.BlockSpec(memory_space=pl.ANY),
                gu_spec(lo), gu_spec(lo), dn_spec(lo),
                gu_spec(hi), gu_spec(hi), dn_spec(hi),
                pl.BlockSpec((D_MODEL, D_MODEL), const2),
                pl.BlockSpec((D_PLE, D_MODEL), const2),
                pl.BlockSpec((2, D_MODEL), const2),
                pl.BlockSpec((2, D_MODEL), const2),
            ],
            out_specs=pl.BlockSpec(memory_space=pl.ANY),
            scratch_shapes=[pltpu.VMEM((2, TM, ROW_W), F32), pltpu.VMEM((2, TM, D_MODEL), F32),
                            pltpu.SemaphoreType.DMA((2,)), pltpu.SemaphoreType.DMA((2,))],
        ),
        out_shape=jax.ShapeDtypeStruct((ntok, D_MODEL), F32),
        compiler_params=pltpu.CompilerParams(dimension_semantics=("arbitrary",),
                                             vmem_limit_bytes=VMEM_LIMIT),
        name="moe_ple",
    )(slot_tok, elo, ehi, nrows, rows, wg, wu, wd, wg, wu, wd, wpg, wpp, lng2, lnb2)


def _route_plan(cls):
    ntok = cls.shape[0]
    n_tiles = -(-(ntok + N_CLASSES * (TM - 1)) // TM)
    ns = n_tiles * TM
    onehot = (cls[:, None] == jnp.arange(N_CLASSES, dtype=jnp.int32)[None, :]).astype(jnp.int32)
    csum = jnp.cumsum(onehot, axis=0)
    rank = jnp.sum(csum * onehot, axis=1) - 1
    counts = csum[-1]
    padded = ((counts + TM - 1) // TM) * TM
    ends = jnp.cumsum(padded)
    starts = ends - padded
    pos = jnp.sum(onehot * starts[None, :], axis=1) + rank
    tok = jnp.arange(ntok, dtype=jnp.int32)
    src = jnp.zeros((ns,), jnp.int32).at[pos].set(tok)
    tile_start = jnp.arange(n_tiles, dtype=jnp.int32) * TM
    tile_cls = jnp.sum((tile_start[:, None] >= ends[None, :]).astype(jnp.int32), axis=1)
    last_cls = jnp.max(jnp.where(counts > 0, jnp.arange(N_CLASSES, dtype=jnp.int32), 0))
    tile_cls = jnp.minimum(tile_cls, last_cls)
    nrows = jnp.clip((starts + counts)[tile_cls] - tile_start, 0, TM)
    grp = tile_cls // len(PAIRS)
    pidx = tile_cls % len(PAIRS)
    p0 = jnp.asarray([p[0] for p in PAIRS], jnp.int32)[pidx]
    p1 = jnp.asarray([p[1] for p in PAIRS], jnp.int32)[pidx]
    return src, grp * EXPERTS_PER_GROUP + p0, grp * EXPERTS_PER_GROUP + p1, nrows


N_CHUNK = D_MODEL // LANES


def _store_residues(val, scr, out_refs, dils):
    if any(d > 1 for d in dils):
        for c in range(N_CHUNK):
            scr[c] = val[:, c * LANES:(c + 1) * LANES]
    for dil, ref in zip(dils, out_refs):
        if dil == 1:
            ref[0] = val.astype(ref.dtype)
            continue
        for r in range(dil):
            for c in range(N_CHUNK):
                ref[r, :, c * LANES:(c + 1) * LANES] = (
                    scr.at[c][pl.ds(r, TT // dil, stride=dil), :].astype(ref.dtype))


ALL_DILS = tuple(d for _, d in DIL_PATTERNS)


def _kv_prompt_kernel(x_ref, w_ref, kf_ref, vf_ref, k0_ref, k1_ref, k2_ref, v0_ref, v1_ref, v2_ref, scr):
    kv = jnp.dot(x_ref[...].astype(BF16), w_ref[...], preferred_element_type=F32)
    k = kv[:, 0:D_MODEL]
    v = kv[:, D_MODEL:2 * D_MODEL]
    kf_ref[...] = k
    vf_ref[...] = v
    _store_residues(k, scr, (k0_ref, k1_ref, k2_ref), ALL_DILS)
    _store_residues(v, scr, (v0_ref, v1_ref, v2_ref), ALL_DILS)


def _q_prompt_kernel(x_ref, w_ref, q0_ref, q1_ref, q2_ref, scr):
    xb = x_ref[...].astype(BF16)
    for g, ref in enumerate((q0_ref, q1_ref, q2_ref)):
        q = jnp.dot(xb, w_ref[:, g * D_MODEL:(g + 1) * D_MODEL], preferred_element_type=F32)
        _store_residues(q * (HEAD_DIM ** -0.5), scr, (ref,), (ALL_DILS[g],))


def _residue_specs(batch, seq_len):
    sps = seq_len // TT
    specs, shapes = [], []
    for _, dil in DIL_PATTERNS:
        specs.append(pl.BlockSpec((None, dil, TT // dil, D_MODEL), lambda s: (s // sps, 0, s % sps, 0)))
        shapes.append(jax.ShapeDtypeStruct((batch, dil, seq_len // dil, D_MODEL), BF16))
    return specs, shapes


def _kv_proj_prompt(x, w, *, batch, seq_len):
    n_prompt = batch * seq_len
    blk = lambda s: (s, 0)
    specs, shapes = _residue_specs(batch, seq_len)
    return pl.pallas_call(
        _kv_prompt_kernel,
        grid=(n_prompt // TT,),
        in_specs=[pl.BlockSpec((TT, D_MODEL), blk), pl.BlockSpec((D_MODEL, 2 * D_MODEL), lambda s: (0, 0))],
        out_specs=[pl.BlockSpec((TT, D_MODEL), blk)] * 2 + specs + specs,
        out_shape=[jax.ShapeDtypeStruct((n_prompt, D_MODEL), F32)] * 2 + shapes + shapes,
        scratch_shapes=[pltpu.VMEM((N_CHUNK, TT, LANES), F32)],
        compiler_params=pltpu.CompilerParams(dimension_semantics=("arbitrary",),
                                             vmem_limit_bytes=VMEM_LIMIT),
        name="kv_proj_prompt",
    )(x, w)


def _q_proj_prompt(x, w, *, batch, seq_len):
    n_prompt = batch * seq_len
    specs, shapes = _residue_specs(batch, seq_len)
    return pl.pallas_call(
        _q_prompt_kernel,
        grid=(n_prompt // TT,),
        in_specs=[pl.BlockSpec((TT, D_MODEL), lambda s: (s, 0)),
                  pl.BlockSpec((D_MODEL, N_DIL * D_MODEL), lambda s: (0, 0))],
        out_specs=specs,
        out_shape=shapes,
        scratch_shapes=[pltpu.VMEM((N_CHUNK, TT, LANES), F32)],
        compiler_params=pltpu.CompilerParams(dimension_semantics=("arbitrary",),
                                             vmem_limit_bytes=VMEM_LIMIT),
        name="q_proj_prompt",
    )(x, w)


def _proj_sample_kernel(x_ref, w_ref, o_ref, *, scale):
    o_ref[...] = jnp.dot(x_ref[...].astype(BF16), w_ref[...], preferred_element_type=F32) * scale


def _proj_sample(x, w, *, n_prompt, n_sample, scale):
    n_out = w.shape[1]
    first = n_prompt // TT
    return pl.pallas_call(
        functools.partial(_proj_sample_kernel, scale=scale),
        grid=(n_sample // TT, n_out // D_MODEL),
        in_specs=[pl.BlockSpec((TT, D_MODEL), lambda s, c: (first + s, 0)),
                  pl.BlockSpec((D_MODEL, D_MODEL), lambda s, c: (0, c))],
        out_specs=pl.BlockSpec((TT, D_MODEL), lambda s, c: (s, c)),
        out_shape=jax.ShapeDtypeStruct((n_sample, n_out), F32),
        compiler_params=pltpu.CompilerParams(dimension_semantics=("arbitrary", "arbitrary"),
                                             vmem_limit_bytes=VMEM_LIMIT),
        name="proj_sample",
    )(x, w)


def _attn_prompt_kernel(q_ref, kp_ref, kc_ref, vp_ref, vc_ref, o_ref, lse_ref, *, group):
    i = pl.program_id(2)
    dil = DIL_PATTERNS[group][1]
    a = lax.broadcasted_iota(jnp.int32, (QB, 2 * QB), 0)
    j = lax.broadcasted_iota(jnp.int32, (QB, 2 * QB), 1)
    delta = QB + a - j
    ok = (delta >= 0) & (delta <= QB) & ((j >= QB) | (i > 0))
    dist = (delta * dil).astype(F32)
    lane = lax.broadcasted_iota(jnp.int32, (QB, LANES), 1)
    first_half = lane < HEAD_DIM
    lse_blk = jnp.zeros((QB, LANES), F32)
    for hp in range(N_HEADS // 2):
        cols = pl.ds(hp * LANES, LANES)
        q2 = q_ref[:, cols]
        k2 = jnp.concatenate([kp_ref[:, cols], kc_ref[:, cols]], axis=0)
        v2 = jnp.concatenate([vp_ref[:, cols], vc_ref[:, cols]], axis=0)
        outs = []
        for half in range(2):
            h = 2 * hp + half
            in_head = first_half if half == 0 else jnp.logical_not(first_half)
            qm = jnp.where(in_head, q2, jnp.zeros_like(q2))
            s = lax.dot_general(qm, k2, NT_DIMS, preferred_element_type=F32)
            s = jnp.where(ok, s - float(SLOPES[group, h]) * dist, NEG)
            mx = jnp.max(s, axis=1, keepdims=True)
            pexp = jnp.exp(s - mx)
            den = jnp.sum(pexp, axis=1, keepdims=True)
            pv = jnp.dot(pexp.astype(BF16), v2, preferred_element_type=F32)
            outs.append(pv / den)
            lse_blk = jnp.where(lane == h, mx + jnp.log(den), lse_blk)
        o_ref[:, cols] = jnp.where(first_half, outs[0], outs[1]).astype(o_ref.dtype)
    lse_ref[...] = lse_blk


def _attn_prompt(qg, kg, vg, *, group):
    batch, dil, rows, _ = qg.shape
    nq = rows // QB
    cur = lambda b, r, i: (b, r, i, 0)
    prev = lambda b, r, i: (b, r, jnp.maximum(i - 1, 0), 0)
    blk = lambda im: pl.BlockSpec((None, None, QB, D_MODEL), im)
    return pl.pallas_call(
        functools.partial(_attn_prompt_kernel, group=group),
        grid=(batch, dil, nq),
        in_specs=[blk(cur), blk(prev), blk(cur), blk(prev), blk(cur)],
        out_specs=[blk(cur), pl.BlockSpec((None, None, QB, LANES), cur)],
        out_shape=[jax.ShapeDtypeStruct((batch, dil, rows, D_MODEL), BF16),
                   jax.ShapeDtypeStruct((batch, dil, rows, LANES), F32)],
        compiler_params=pltpu.CompilerParams(dimension_semantics=("arbitrary",) * 3,
                                             vmem_limit_bytes=VMEM_LIMIT),
        name=f"attn_prompt_g{group}",
    )(qg, kg, kg, vg, vg)


def _attn_sample_kernel(q_ref, kt_ref, vt_ref, kn_ref, vn_ref, sl_ref, o_ref, *shift_refs, dec_seq):
    n_heads = q_ref.shape[0]
    nrow = N_DIL * dec_seq
    ncol = W_MAX + LANES
    new0 = LANES - dec_seq
    row = lax.broadcasted_iota(jnp.int32, (nrow, ncol), 0)
    col = lax.broadcasted_iota(jnp.int32, (nrow, ncol), 1)
    grp = row // dec_seq
    kpos = jnp.where(col < W_MAX, col, col - new0)
    dist = W_MAX + (row - grp * dec_seq) - kpos
    win = jnp.where(grp == 0, DIL_PATTERNS[0][0], jnp.where(grp == 1, DIL_PATTERNS[1][0], DIL_PATTERNS[2][0]))
    dmask = jnp.where(grp == 0, DIL_PATTERNS[0][1] - 1,
                      jnp.where(grp == 1, DIL_PATTERNS[1][1] - 1, DIL_PATTERNS[2][1] - 1))
    ok = ((col < W_MAX) | (col >= W_MAX + new0)) & (dist >= 0) & (dist <= win) & ((dist & dmask) == 0)
    distf = dist.astype(F32)
    lane = lax.broadcasted_iota(jnp.int32, (dec_seq, LANES), 1)

    def scores(h):
        q = q_ref[h].astype(BF16)
        s = jnp.concatenate([jnp.dot(q, kt_ref[h].astype(BF16), preferred_element_type=F32),
                             jnp.dot(q, kn_ref[h].astype(BF16), preferred_element_type=F32)], axis=1)
        s = jnp.where(ok, s - sl_ref[h][:, 0:1] * distf, NEG)
        mx = jnp.max(s, axis=1, keepdims=True)
        pexp = jnp.exp(s - mx)
        den = jnp.sum(pexp, axis=1, keepdims=True)
        return pexp.astype(BF16), den, mx + jnp.log(den)

    def mix_groups(o, lse):
        parts = [o[g * dec_seq:(g + 1) * dec_seq] for g in range(N_DIL)]
        ls = [lse[g * dec_seq:(g + 1) * dec_seq] for g in range(N_DIL)]
        top = jnp.maximum(jnp.maximum(ls[0], ls[1]), ls[2])
        ws = [jnp.exp(l - top) for l in ls]
        return (ws[0] * parts[0] + ws[1] * parts[1] + ws[2] * parts[2]) / (ws[0] + ws[1] + ws[2])

    for hp in range(n_heads // 2):
        vt2 = vt_ref[2 * hp:2 * hp + 2].reshape(2 * HEAD_DIM, W_MAX).astype(BF16)
        vn2 = vn_ref[2 * hp:2 * hp + 2].reshape(2 * HEAD_DIM, LANES).astype(BF16)
        mixed = []
        for half in range(2):
            pb, den, lse = scores(2 * hp + half)
            acc = (lax.dot_general(pb[:, 0:W_MAX], vt2, NT_DIMS, preferred_element_type=F32)
                   + lax.dot_general(pb[:, W_MAX:ncol], vn2, NT_DIMS, preferred_element_type=F32))
            mixed.append(mix_groups(acc / den, lse))
        o_ref[:, hp * LANES:(hp + 1) * LANES] = jnp.where(lane < HEAD_DIM, mixed[0], mixed[1])

    if shift_refs:
        lane_s = lax.broadcasted_iota(jnp.int32, (HEAD_DIM, LANES), 1)
        n_chunks = W_MAX // LANES
        for src_ref, new_ref, dst_ref in ((kt_ref, kn_ref, shift_refs[0]), (vt_ref, vn_ref, shift_refs[1])):
            for h in range(n_heads):
                rolled = [pltpu.roll(src_ref[h, :, c * LANES:(c + 1) * LANES], new0, 1) for c in range(n_chunks)]
                rolled.append(new_ref[h])
                for c in range(n_chunks):
                    dst_ref[h, :, c * LANES:(c + 1) * LANES] = jnp.where(lane_s < new0, rolled[c], rolled[c + 1])


def _attn_sample(q4, kt, vt, knp, vnp, slope4, *, dec_seq, heads_per_step, shift):
    nb = q4.shape[0]
    hps = heads_per_step
    nrow = N_DIL * dec_seq
    per_head = lambda last2: pl.BlockSpec((None, hps) + last2, lambda b, c: (b, c, 0, 0))
    out_specs = [pl.BlockSpec((None, dec_seq, hps * HEAD_DIM), lambda b, c: (b, 0, c))]
    out_shape = [jax.ShapeDtypeStruct((nb, dec_seq, D_MODEL), F32)]
    if shift:
        out_specs += [per_head((HEAD_DIM, W_MAX))] * 2
        out_shape += [jax.ShapeDtypeStruct(kt.shape, kt.dtype)] * 2
    return pl.pallas_call(
        functools.partial(_attn_sample_kernel, dec_seq=dec_seq),
        grid=(nb, N_HEADS // hps),
        in_specs=[per_head((nrow, HEAD_DIM)), per_head((HEAD_DIM, W_MAX)), per_head((HEAD_DIM, W_MAX)),
                  per_head((HEAD_DIM, LANES)), per_head((HEAD_DIM, LANES)),
                  pl.BlockSpec((hps, nrow, LANES), lambda b, c: (c, 0, 0))],
        out_specs=out_specs,
        out_shape=out_shape,
        compiler_params=pltpu.CompilerParams(dimension_semantics=("arbitrary", "arbitrary"),
                                             vmem_limit_bytes=VMEM_LIMIT),
        name="attn_sample_shift" if shift else "attn_sample",
    )(q4, kt, vt, knp, vnp, slope4)


def _attn_out_kernel(x_ref, o0_ref, o1_ref, o2_ref, l0_ref, l1_ref, l2_ref, os_ref, e_ref, wo_ref, p_ref,
                     lng_ref, lnb_ref, wrt_ref, br_ref, r_ref, cls_ref, nat_o, nat_l, *, n_prompt_steps):
    s = pl.program_id(0)

    def finish(o):
        mix = jnp.dot(o.astype(BF16), wo_ref[...], preferred_element_type=F32)
        _tail(x_ref[...], mix, p_ref[...], lng_ref[...], lnb_ref[...], wrt_ref, br_ref, r_ref, cls_ref)

    def natural(ref, scr):
        dil = ref.shape[0]
        if dil == 1:
            return ref[0].astype(F32)
        n_chunk = ref.shape[2] // LANES
        for r in range(dil):
            v = ref[r].astype(F32)
            for c in range(n_chunk):
                scr.at[c][pl.ds(r, TT // dil, stride=dil), :] = v[:, c * LANES:(c + 1) * LANES]
        return jnp.concatenate([scr[c] for c in range(n_chunk)], axis=1)

    @pl.when(s < n_prompt_steps)
    def _():
        lses = [natural(l_ref, nat_l.at[g]) for g, l_ref in enumerate((l0_ref, l1_ref, l2_ref))]
        top = jnp.maximum(jnp.maximum(lses[0], lses[1]), lses[2])
        ws = [jnp.exp(l - top) for l in lses]
        tot = ws[0] + ws[1] + ws[2]
        o = jnp.zeros((TT, D_MODEL), F32)
        for wgt, o_ref in zip(ws, (o0_ref, o1_ref, o2_ref)):
            hi, lo = _split_bf16(wgt / tot)
            wide = (jnp.dot(hi, e_ref[...], preferred_element_type=F32)
                    + jnp.dot(lo, e_ref[...], preferred_element_type=F32))
            o = o + wide * natural(o_ref, nat_o)
        finish(o)

    @pl.when(s >= n_prompt_steps)
    def _():
        finish(os_ref[...])


def _attn_out_layer(x, o3, l3, o_s, expand, wo, p_flat, lng, lnb, wrt, br, *, n_prompt, seq_len):
    ntok = p_flat.shape[0]
    npst = n_prompt // TT
    sps = seq_len // TT
    blk = lambda s: (s, 0)
    sblk = lambda s: (jnp.maximum(s - npst, 0), 0)
    const = lambda s: (0, 0)

    def res_spec(dil, width):
        def im(s):
            sp = jnp.minimum(s, npst - 1)
            return (sp // sps, 0, sp % sps, 0)
        return pl.BlockSpec((None, dil, TT // dil, width), im)

    return pl.pallas_call(
        functools.partial(_attn_out_kernel, n_prompt_steps=npst),
        grid=(ntok // TT,),
        in_specs=[pl.BlockSpec((TT, D_MODEL), blk)]
        + [res_spec(dil, D_MODEL) for _, dil in DIL_PATTERNS]
        + [res_spec(dil, LANES) for _, dil in DIL_PATTERNS]
        + [pl.BlockSpec((TT, D_MODEL), sblk),
           pl.BlockSpec((LANES, D_MODEL), const),
           pl.BlockSpec((D_MODEL, D_MODEL), const),
           pl.BlockSpec((TT, D_PLE), blk),
           pl.BlockSpec((1, D_MODEL), const),
           pl.BlockSpec((1, D_MODEL), const),
           pl.BlockSpec((N_EXPERTS, D_MODEL), const),
           pl.BlockSpec((N_EXPERTS, 1), const)],
        out_specs=[pl.BlockSpec((TT, ROW_W), blk), pl.BlockSpec((1, TT), lambda s: (0, s))],
        out_shape=[jax.ShapeDtypeStruct((ntok, ROW_W), F32), jax.ShapeDtypeStruct((1, ntok), jnp.int32)],
        scratch_shapes=[pltpu.VMEM((N_CHUNK, TT, LANES), F32), pltpu.VMEM((N_DIL, 1, TT, LANES), F32)],
        compiler_params=pltpu.CompilerParams(dimension_semantics=("arbitrary",),
                                             vmem_limit_bytes=VMEM_LIMIT),
        name="attn_out",
    )(x, *o3, *l3, o_s, expand, wo, p_flat, lng, lnb, wrt, br)


def kernel(x_prompt, x_sample, state_pool, cache_k, cache_v, p_prompt, p_sample, w_pool, pool_scale, w_kv,
           w_q, w_o, ln_g, ln_b, w_router, b_router, w_exp_gate, w_exp_up, w_exp_down, w_ple_gate,
           w_ple_proj):
    batch, seq_len, d = x_prompt.shape
    nb, dec_seq, _ = x_sample.shape
    wbuf = cache_k.shape[1]
    n_prompt = batch * seq_len
    n_sample = nb * dec_seq
    ntok = n_prompt + n_sample
    assert d == D_MODEL and wbuf == W_MAX and dec_seq == 8
    assert seq_len % (QB * DIL_PATTERNS[2][1]) == 0 and n_sample % TT == 0 and seq_len % TT == 0

    wrt2 = w_router.T.astype(BF16)
    br = b_router.astype(F32).reshape(N_EXPERTS, 1)
    wg, wu, wd = w_exp_gate.astype(BF16), w_exp_up.astype(BF16), w_exp_down.astype(BF16)
    wpg, wpp = w_ple_gate.astype(BF16), w_ple_proj.astype(BF16)
    wp = w_pool.astype(BF16)
    head_of_col = np.arange(D_MODEL) // HEAD_DIM
    expand = jnp.asarray(np.arange(LANES)[:, None] == head_of_col[None, :], BF16)
    slope4 = jnp.asarray(np.ascontiguousarray(np.broadcast_to(
        np.repeat(SLOPES.T, dec_seq, axis=1)[:, :, None], (N_HEADS, N_DIL * dec_seq, LANES))))

    p_flat = jnp.concatenate([p_prompt.reshape(DEPTH, n_prompt, D_PLE),
                              p_sample.reshape(DEPTH, n_sample, D_PLE)], axis=1)

    def moe_stage(rows, cls, layer):
        slot_tok, elo, ehi, nrows = _route_plan(cls.reshape(ntok))
        return _moe_layer(rows, slot_tok, elo, ehi, nrows, wg, wu, wd, wpg[layer], wpp[layer],
                          ln_g[layer, 1:3], ln_b[layer, 1:3], layer=layer)

    def to_heads_minor(a, width):
        a = a.reshape(nb, dec_seq, N_HEADS, HEAD_DIM).transpose(0, 2, 3, 1)
        return jnp.pad(a, ((0, 0), (0, 0), (0, 0), (width - dec_seq, 0)))

    x = x_prompt.reshape(n_prompt, D_MODEL)
    xs = x_sample
    pool_p, pool_s = [], []
    for i in range(N_A):
        ext_s = jnp.concatenate([jnp.zeros((nb, HALO - POOL_HIST, D_MODEL), F32), state_pool[i], xs], axis=1)
        pool_p.append(x[:n_prompt].reshape(batch, seq_len, D_MODEL)[:, seq_len - POOL_HIST:])
        pool_s.append(ext_s[:, HALO + dec_seq - POOL_HIST:])
        rows, cls = _pool_layer(x, ext_s, p_flat[i], wp[i], pool_scale[i].reshape(1, D_MODEL),
                                ln_g[i, 0].reshape(1, D_MODEL), ln_b[i, 0].reshape(1, D_MODEL), wrt2, br,
                                n_prompt=n_prompt, seq_len=seq_len)
        x = moe_stage(rows, cls, i)
        xs = x[n_prompt:ntok].reshape(nb, dec_seq, D_MODEL)

    w_kv_b = w_kv.astype(BF16)
    kf, vf, *kv_res = _kv_proj_prompt(x, w_kv_b, batch=batch, seq_len=seq_len)
    k_res, v_res = kv_res[:N_DIL], kv_res[N_DIL:]
    kv_s = _proj_sample(x, w_kv_b, n_prompt=n_prompt, n_sample=n_sample, scale=1.0)
    knp = to_heads_minor(kv_s[:, :D_MODEL], LANES)
    vnp = to_heads_minor(kv_s[:, D_MODEL:], LANES)
    kt = cache_k.transpose(0, 2, 3, 1)
    vt = cache_v.transpose(0, 2, 3, 1)

    for jl in range(N_B):
        i = N_A + jl
        w_q_b = w_q[jl].astype(BF16)
        q_res = _q_proj_prompt(x, w_q_b, batch=batch, seq_len=seq_len)
        o3, l3 = [], []
        for g in range(N_DIL):
            o_g, l_g = _attn_prompt(q_res[g], k_res[g], v_res[g], group=g)
            o3.append(o_g)
            l3.append(l_g)
        q_s = _proj_sample(x, w_q_b, n_prompt=n_prompt, n_sample=n_sample, scale=HEAD_DIM ** -0.5)
        q4 = (q_s.reshape(nb, dec_seq, N_DIL, N_HEADS, HEAD_DIM)
              .transpose(0, 3, 2, 1, 4).reshape(nb, N_HEADS, N_DIL * dec_seq, HEAD_DIM))
        if jl == 0:
            o_s, kt_new, vt_new = _attn_sample(q4, kt, vt, knp, vnp, slope4, dec_seq=dec_seq,
                                               heads_per_step=4, shift=True)
        else:
            (o_s,) = _attn_sample(q4, kt, vt, knp, vnp, slope4, dec_seq=dec_seq, heads_per_step=8, shift=False)
        rows, cls = _attn_out_layer(x, o3, l3, o_s.reshape(n_sample, D_MODEL), expand, w_o[jl].astype(BF16),
                                    p_flat[i], ln_g[i, 0].reshape(1, D_MODEL), ln_b[i, 0].reshape(1, D_MODEL),
                                    wrt2, br, n_prompt=n_prompt, seq_len=seq_len)
        x = moe_stage(rows, cls, i)

    keep_p = max(seq_len - W_MAX, 0)
    hd = (N_HEADS, HEAD_DIM)
    k_prompt = kf.reshape(batch, seq_len, *hd)[:, keep_p:]
    v_prompt = vf.reshape(batch, seq_len, *hd)[:, keep_p:]
    return (x[:n_prompt].reshape(batch, seq_len, D_MODEL),
            x[n_prompt:ntok].reshape(nb, dec_seq, D_MODEL),
            jnp.stack(pool_p, 0), jnp.stack(pool_s, 0),
            k_prompt, v_prompt,
            kt_new.transpose(0, 3, 1, 2), vt_new.transpose(0, 3, 1, 2))
```

```python
import functools

import numpy as np
import jax
import jax.numpy as jnp
from jax import lax
from jax.experimental import pallas as pl
from jax.experimental.pallas import tpu as pltpu

D_MODEL = 1024
DEPTH = 4
N_A = DEPTH // 2
N_B = DEPTH - N_A
POOL_WINDOWS = (2, 4, 8, 16)
POOL_GROUP = D_MODEL // len(POOL_WINDOWS)
POOL_HIST = max(POOL_WINDOWS) - 1
HEAD_DIM = 64
N_HEADS = D_MODEL // HEAD_DIM
DIL_PATTERNS = ((128, 1), (512, 4), (2048, 16))
N_DIL = len(DIL_PATTERNS)
W_MAX = 2048
N_EXPERTS = 16
N_EXPERT_GROUPS = 4
EXPERTS_PER_GROUP = 4
D_EXPERT = 512
D_PLE = 256
ALPHA = (2 * DEPTH) ** 0.25
LN_EPS = 1e-5

PAIRS = ((0, 1), (0, 2), (0, 3), (1, 2), (1, 3), (2, 3))
N_CLASSES = N_EXPERT_GROUPS * len(PAIRS)

LANES = 128
HALO = 16
TT = 512
TM = 256
DMA_UNROLL = 8
assert D_MODEL == 8 * LANES
QB = 128
ROW_W = D_MODEL + D_PLE + LANES
VMEM_LIMIT = 48 * 1024 * 1024
NEG = -1e30

F32 = jnp.float32
BF16 = jnp.bfloat16
NT_DIMS = (((1,), (1,)), ((), ()))


def _alibi_slopes():
    n = N_DIL * N_HEADS
    return (2.0 ** (-8.0 * np.arange(1, n + 1) / n)).astype(np.float32).reshape(N_DIL, N_HEADS)


SLOPES = _alibi_slopes()


def _ln(x, g, b):
    mu = jnp.mean(x, axis=-1, keepdims=True)
    xc = x - mu
    var = jnp.mean(xc * xc, axis=-1, keepdims=True)
    return xc * lax.rsqrt(var + LN_EPS) * g + b


def _split_bf16(x):
    hi = x.astype(BF16)
    lo = (x - hi.astype(F32)).astype(BF16)
    return hi, lo


def _route(x1, wrt_ref, br_ref):
    z = lax.dot_general(wrt_ref[...], x1.astype(BF16), NT_DIMS, preferred_element_type=F32)
    s = jax.nn.sigmoid(z)
    sel = s + br_ref[...]
    a = [sel[k:k + 1, :] for k in range(N_EXPERTS)]
    sv = [s[k:k + 1, :] for k in range(N_EXPERTS)]
    one = jnp.ones_like(a[0])
    zero = jnp.zeros_like(a[0])
    gscore, chosen = [], []
    for g in range(N_EXPERT_GROUPS):
        ag = a[4 * g:4 * g + 4]
        cnt = []
        for j in range(4):
            c = zero
            for k in range(4):
                if k == j:
                    continue
                beats = (ag[k] > ag[j]) | (ag[k] == ag[j]) if k < j else (ag[k] > ag[j])
                c = c + jnp.where(beats, one, zero)
            cnt.append(c)
        top = zero
        sec = zero
        for j in range(4):
            top = top + jnp.where(cnt[j] == 0.0, ag[j], zero)
            sec = sec + jnp.where(cnt[j] == 1.0, ag[j], zero)
        gscore.append(top + sec)
        chosen.append([jnp.where(cnt[j] < 2.0, one, zero) for j in range(4)])
    best = gscore[0]
    gi = zero
    for g in range(1, N_EXPERT_GROUPS):
        better = gscore[g] > best
        gi = jnp.where(better, float(g), gi)
        best = jnp.where(better, gscore[g], best)
    m = []
    sg = []
    for j in range(4):
        mj = chosen[0][j]
        sj = sv[j]
        for g in range(1, N_EXPERT_GROUPS):
            isg = gi == float(g)
            mj = jnp.where(isg, chosen[g][j], mj)
            sj = jnp.where(isg, sv[4 * g + j], sj)
        m.append(mj)
        sg.append(sj)
    pidx = zero
    s_lo = zero
    s_hi = zero
    for idx, (p0, p1) in enumerate(PAIRS):
        hit = (m[p0] * m[p1]) > 0.5
        pidx = jnp.where(hit, float(idx), pidx)
        s_lo = jnp.where(hit, sg[p0], s_lo)
        s_hi = jnp.where(hit, sg[p1], s_hi)
    den = s_lo + s_hi
    cls = (gi * float(len(PAIRS)) + pidx).astype(jnp.int32)
    return s_lo / den, s_hi / den, cls


def _tail(x, mix, p, lng, lnb, wrt_ref, br_ref, r_ref, cls_ref):
    x1 = _ln(ALPHA * x + mix, lng, lnb)
    g_lo, g_hi, cls = _route(x1, wrt_ref, br_ref)
    n = x.shape[0]
    row = lax.broadcasted_iota(jnp.int32, (LANES, n), 0)
    meta_t = jnp.where(row == 0, g_lo, jnp.where(row == 1, g_hi, 0.0))
    r_ref[:, 0:D_MODEL] = x1
    r_ref[:, D_MODEL:D_MODEL + D_PLE] = p
    r_ref[:, D_MODEL + D_PLE:ROW_W] = meta_t.T
    cls_ref[...] = cls


def _pool_kernel(x_ref, halo_ref, exts_ref, p_ref, wp_ref, sc_ref, lng_ref, lnb_ref, wrt_ref, br_ref,
                 r_ref, cls_ref, ext_scr, *, n_prompt_steps, steps_per_seq):
    s = pl.program_id(0)

    def finish(x, diff_parts, p):
        ys = [jnp.dot(diff_parts[g].astype(BF16), wp_ref[g], preferred_element_type=F32)
              for g in range(len(POOL_WINDOWS))]
        mix = jnp.concatenate(ys, axis=1) * sc_ref[...]
        _tail(x, mix, p, lng_ref[...], lnb_ref[...], wrt_ref, br_ref, r_ref, cls_ref)

    @pl.when(s < n_prompt_steps)
    def _():
        t = s % steps_per_seq
        keep = jnp.where(t > 0, 1.0, 0.0)
        ext_scr[0:HALO, :] = halo_ref[...] * keep
        ext_scr[HALO:HALO + TT, :] = x_ref[...]
        tpos = (t * TT + lax.broadcasted_iota(jnp.int32, (TT, 1), 0)).astype(F32)
        parts = []
        for g, w in enumerate(POOL_WINDOWS):
            cols = pl.ds(g * POOL_GROUP, POOL_GROUP)
            xg = ext_scr[pl.ds(HALO, TT), cols]
            acc = xg
            for k in range(1, w):
                acc = acc + ext_scr[pl.ds(HALO - k, TT), cols]
            cnt = jnp.minimum(float(w), tpos + 1.0)
            parts.append(acc / cnt - xg)
        finish(x_ref[...], parts, p_ref[...])

    @pl.when(s >= n_prompt_steps)
    def _():
        nb = exts_ref.shape[0]
        hist = exts_ref.shape[1] - HALO
        parts = []
        for g, w in enumerate(POOL_WINDOWS):
            cols = pl.ds(g * POOL_GROUP, POOL_GROUP)
            xg = exts_ref[:, pl.ds(HALO, hist), cols]
            acc = xg
            for k in range(1, w):
                acc = acc + exts_ref[:, pl.ds(HALO - k, hist), cols]
            parts.append((acc / float(w) - xg).reshape(nb * hist, POOL_GROUP))
        x = exts_ref[:, pl.ds(HALO, hist), :].reshape(nb * hist, D_MODEL)
        finish(x, parts, p_ref[...])


def _pool_layer(xp_flat, ext_s, p_flat, wp, sc, lng, lnb, wrt, br, *, n_prompt, seq_len):
    ntok = p_flat.shape[0]
    n_steps = ntok // TT
    npst = n_prompt // TT
    sps = seq_len // TT
    dec_seq = ext_s.shape[1] - HALO
    nb = TT // dec_seq
    kern = functools.partial(_pool_kernel, n_prompt_steps=npst, steps_per_seq=sps)
    const = lambda s: (0, 0)
    return pl.pallas_call(
        kern,
        grid=(n_steps,),
        in_specs=[
            pl.BlockSpec((TT, D_MODEL), lambda s: (jnp.minimum(s, npst - 1), 0)),
            pl.BlockSpec((HALO, D_MODEL),
                         lambda s: (jnp.maximum(jnp.minimum(s, npst - 1) * (TT // HALO) - 1, 0), 0)),
            pl.BlockSpec((nb, HALO + dec_seq, D_MODEL), lambda s: (jnp.maximum(s - npst, 0), 0, 0)),
            pl.BlockSpec((TT, D_PLE), lambda s: (s, 0)),
            pl.BlockSpec((len(POOL_WINDOWS), POOL_GROUP, POOL_GROUP), lambda s: (0, 0, 0)),
            pl.BlockSpec((1, D_MODEL), const),
            pl.BlockSpec((1, D_MODEL), const),
            pl.BlockSpec((1, D_MODEL), const),
            pl.BlockSpec((N_EXPERTS, D_MODEL), const),
            pl.BlockSpec((N_EXPERTS, 1), const),
        ],
        out_specs=[
            pl.BlockSpec((TT, ROW_W), lambda s: (s, 0)),
            pl.BlockSpec((1, TT), lambda s: (0, s)),
        ],
        out_shape=[
            jax.ShapeDtypeStruct((ntok, ROW_W), F32),
            jax.ShapeDtypeStruct((1, ntok), jnp.int32),
        ],
        scratch_shapes=[pltpu.VMEM((HALO + TT, D_MODEL), F32)],
        compiler_params=pltpu.CompilerParams(dimension_semantics=("arbitrary",),
                                             vmem_limit_bytes=VMEM_LIMIT),
        name="pool_mixer",
    )(xp_flat, xp_flat, ext_s, p_flat, wp, sc, lng, lnb, wrt, br)


def _moe_kernel(gidx_ref, sidx_ref, elo_ref, ehi_ref, nused_ref, r_hbm, wg_lo, wu_lo, wd_lo, wg_hi, wu_hi, wd_hi,
                wpg_ref, wpp_ref, lng_ref, lnb_ref, x_hbm, rbuf_a, rbuf_b, ybuf_a, ybuf_b, gsem, ssem,
                *, n_tiles):
    k = pl.program_id(0)

    def gather_start(tile, rbuf, sem, unrolled):
        def one(j):
            pltpu.make_async_copy(r_hbm.at[pl.ds(gidx_ref[tile * TM + j], 1)], rbuf.at[pl.ds(j, 1)], sem).start()
        if unrolled:
            for j in range(TM):
                one(j)
        else:
            lax.fori_loop(0, TM, lambda j, c: (one(j), c)[1], 0, unroll=DMA_UNROLL)

    def gather_wait(rbuf, sem):
        pltpu.make_async_copy(r_hbm.at[pl.ds(0, TM)], rbuf, sem).wait()

    def scatter_start(tile, ybuf, sem, unrolled):
        def one(j):
            pltpu.make_async_copy(ybuf.at[pl.ds(j, 1)], x_hbm.at[pl.ds(sidx_ref[(tile + 1) * TM + j], 1)],
                                  sem).start()
        if unrolled:
            for j in range(TM):
                one(j)
        else:
            lax.fori_loop(0, TM, lambda j, c: (one(j), c)[1], 0, unroll=DMA_UNROLL)

    def scatter_wait(ybuf, sem):
        pltpu.make_async_copy(ybuf, x_hbm.at[pl.ds(0, TM)], sem).wait()

    def step(rbuf, ybuf, own, rbuf_o, ybuf_o, oth, used):
        gather_wait(rbuf, gsem.at[own])
        gather_start(jnp.minimum(k + 1, n_tiles - 1), rbuf_o, gsem.at[oth], used)
        scatter_start(k - 1, ybuf_o, ssem.at[oth], used)
        if not used:
            scatter_wait(ybuf_o, ssem.at[oth])
            return
        x1 = rbuf[:, 0:D_MODEL]
        p = rbuf[:, D_MODEL:D_MODEL + D_PLE]
        gates = rbuf[:, D_MODEL + D_PLE:ROW_W]
        xb = x1.astype(BF16)

        def expert(wg, wu, wd):
            hg = jnp.dot(xb, wg[...], preferred_element_type=F32)
            hu = jnp.dot(xb, wu[...], preferred_element_type=F32)
            h = (hg * jax.nn.sigmoid(hg)) * hu
            return jnp.dot(h.astype(BF16), wd[...], preferred_element_type=F32)

        moe = gates[:, 0:1] * expert(wg_lo, wu_lo, wd_lo)
        moe = moe + gates[:, 1:2] * expert(wg_hi, wu_hi, wd_hi)
        x2 = _ln(ALPHA * x1 + moe, lng_ref[0:1, :], lnb_ref[0:1, :])
        gate = jax.nn.sigmoid(jnp.dot(x2.astype(BF16), wpg_ref[...], preferred_element_type=F32))
        proj = jnp.dot(p.astype(BF16), wpp_ref[...], preferred_element_type=F32)
        ybuf[...] = _ln(ALPHA * x2 + gate * proj, lng_ref[1:2, :], lnb_ref[1:2, :])
        scatter_wait(ybuf_o, ssem.at[oth])

    @pl.when(k == 0)
    def _():
        gather_start(0, rbuf_a, gsem.at[0], False)
        ybuf_b[...] = jnp.zeros_like(ybuf_b)

    is_used = k < nused_ref[0]
    for used in (True, False):
        cond = is_used if used else jnp.logical_not(is_used)

        @pl.when(cond & (k % 2 == 0))
        def _():
            step(rbuf_a, ybuf_a, 0, rbuf_b, ybuf_b, 1, used)

        @pl.when(cond & (k % 2 == 1))
        def _():
            step(rbuf_b, ybuf_b, 1, rbuf_a, ybuf_a, 0, used)

    @pl.when(k == n_tiles - 1)
    def _():
        own = (n_tiles - 1) % 2
        rbuf, ybuf = (rbuf_a, ybuf_a) if own == 0 else (rbuf_b, ybuf_b)
        rbuf_o = rbuf_b if own == 0 else rbuf_a
        scatter_start(k, ybuf, ssem.at[own], False)
        scatter_wait(ybuf, ssem.at[own])
        gather_wait(rbuf_o, gsem.at[1 - own])


def _moe_layer(rows, gidx, sidx, elo, ehi, nused, wg, wu, wd, wpg, wpp, lng2, lnb2, *, layer):
    ntok = rows.shape[0]
    n_tiles = elo.shape[0]
    lo = lambda k, gidx, sidx, elo, ehi, nused: (layer, elo[k], 0, 0)
    hi = lambda k, gidx, sidx, elo, ehi, nused: (layer, ehi[k], 0, 0)
    const2 = lambda k, gidx, sidx, elo, ehi, nused: (0, 0)
    gu_spec = lambda im: pl.BlockSpec((None, None, D_MODEL, D_EXPERT), im)
    dn_spec = lambda im: pl.BlockSpec((None, None, D_EXPERT, D_MODEL), im)
    return pl.pallas_call(
        functools.partial(_moe_kernel, n_tiles=n_tiles),
        grid_spec=pltpu.PrefetchScalarGridSpec(
            num_scalar_prefetch=5,
            grid=(n_tiles,),
            in_specs=[
                pl.BlockSpec(memory_space=pl.ANY),
                gu_spec(lo), gu_spec(lo), dn_spec(lo),
                gu_spec(hi), gu_spec(hi), dn_spec(hi),
                pl.BlockSpec((D_MODEL, D_MODEL), const2),
                pl.BlockSpec((D_PLE, D_MODEL), const2),
                pl.BlockSpec((2, D_MODEL), const2),
                pl.BlockSpec((2, D_MODEL), const2),
            ],
            out_specs=pl.BlockSpec(memory_space=pl.ANY),
            scratch_shapes=[pltpu.VMEM((TM, ROW_W), F32), pltpu.VMEM((TM, ROW_W), F32),
                            pltpu.VMEM((TM, D_MODEL), F32), pltpu.VMEM((TM, D_MODEL), F32),
                            pltpu.SemaphoreType.DMA((2,)), pltpu.SemaphoreType.DMA((2,))],
        ),
        out_shape=jax.ShapeDtypeStruct((ntok + TM, D_MODEL), F32),
        compiler_params=pltpu.CompilerParams(dimension_semantics=("arbitrary",),
                                             vmem_limit_bytes=VMEM_LIMIT),
        name="moe_ple",
    )(gidx, sidx, elo, ehi, nused, rows, wg, wu, wd, wg, wu, wd, wpg, wpp, lng2, lnb2)


def _route_plan(cls):
    ntok = cls.shape[0]
    n_tiles = -(-(ntok + N_CLASSES * (TM - 1)) // TM)
    ns = n_tiles * TM
    onehot = (cls[:, None] == jnp.arange(N_CLASSES, dtype=jnp.int32)[None, :]).astype(jnp.int32)
    csum = jnp.cumsum(onehot, axis=0)
    rank = jnp.sum(csum * onehot, axis=1) - 1
    counts = csum[-1]
    padded = ((counts + TM - 1) // TM) * TM
    ends = jnp.cumsum(padded)
    starts = ends - padded
    pos = jnp.sum(onehot * starts[None, :], axis=1) + rank
    tok = jnp.arange(ntok, dtype=jnp.int32)
    gidx = jnp.zeros((ns,), jnp.int32).at[pos].set(tok)
    sidx = (ntok + jnp.arange(ns + TM, dtype=jnp.int32) % TM).at[pos + TM].set(tok)
    tile_start = jnp.arange(n_tiles, dtype=jnp.int32) * TM
    tile_cls = jnp.sum((tile_start[:, None] >= ends[None, :]).astype(jnp.int32), axis=1)
    last_cls = jnp.max(jnp.where(counts > 0, jnp.arange(N_CLASSES, dtype=jnp.int32), 0))
    tile_cls = jnp.minimum(tile_cls, last_cls)
    grp = tile_cls // len(PAIRS)
    pidx = tile_cls % len(PAIRS)
    p0 = jnp.asarray([p[0] for p in PAIRS], jnp.int32)[pidx]
    p1 = jnp.asarray([p[1] for p in PAIRS], jnp.int32)[pidx]
    nused = (ends[-1:] // TM).astype(jnp.int32)
    return gidx, sidx, grp * EXPERTS_PER_GROUP + p0, grp * EXPERTS_PER_GROUP + p1, nused


N_CHUNK = D_MODEL // LANES


def _store_residues(val, scr, out_refs, dils):
    if any(d > 1 for d in dils):
        for c in range(N_CHUNK):
            scr[c] = val[:, c * LANES:(c + 1) * LANES]
    for dil, ref in zip(dils, out_refs):
        if dil == 1:
            ref[0] = val.astype(ref.dtype)
            continue
        for r in range(dil):
            for c in range(N_CHUNK):
                ref[r, :, c * LANES:(c + 1) * LANES] = (
                    scr.at[c][pl.ds(r, TT // dil, stride=dil), :].astype(ref.dtype))


ALL_DILS = tuple(d for _, d in DIL_PATTERNS)


def _kv_prompt_kernel(x_ref, w_ref, kt_ref, vt_ref, k0_ref, k1_ref, k2_ref, v0_ref, v1_ref, v2_ref, scr,
                      *, steps_per_seq, first_kept):
    kv = jnp.dot(x_ref[...].astype(BF16), w_ref[...], preferred_element_type=F32)
    k = kv[:, 0:D_MODEL]
    v = kv[:, D_MODEL:2 * D_MODEL]

    @pl.when(pl.program_id(0) % steps_per_seq >= first_kept)
    def _():
        kt_ref[...] = k.T
        vt_ref[...] = v.T

    _store_residues(k, scr, (k0_ref, k1_ref, k2_ref), ALL_DILS)
    _store_residues(v, scr, (v0_ref, v1_ref, v2_ref), ALL_DILS)


def _q_prompt_kernel(x_ref, w_ref, q0_ref, q1_ref, q2_ref, scr):
    xb = x_ref[...].astype(BF16)
    for g, ref in enumerate((q0_ref, q1_ref, q2_ref)):
        q = jnp.dot(xb, w_ref[:, g * D_MODEL:(g + 1) * D_MODEL], preferred_element_type=F32)
        _store_residues(q * (HEAD_DIM ** -0.5), scr, (ref,), (ALL_DILS[g],))


def _residue_specs(batch, seq_len):
    sps = seq_len // TT
    specs, shapes = [], []
    for _, dil in DIL_PATTERNS:
        specs.append(pl.BlockSpec((None, dil, TT // dil, D_MODEL), lambda s: (s // sps, 0, s % sps, 0)))
        shapes.append(jax.ShapeDtypeStruct((batch, dil, seq_len // dil, D_MODEL), BF16))
    return specs, shapes


def _kv_proj_prompt(x, w, *, batch, seq_len):
    n_prompt = batch * seq_len
    sps = seq_len // TT
    kept = min(W_MAX, seq_len)
    first_kept = (seq_len - kept) // TT
    specs, shapes = _residue_specs(batch, seq_len)
    t_spec = pl.BlockSpec((None, D_MODEL, TT), lambda s: (s // sps, 0, jnp.maximum(s % sps - first_kept, 0)))
    return pl.pallas_call(
        functools.partial(_kv_prompt_kernel, steps_per_seq=sps, first_kept=first_kept),
        grid=(n_prompt // TT,),
        in_specs=[pl.BlockSpec((TT, D_MODEL), lambda s: (s, 0)),
                  pl.BlockSpec((D_MODEL, 2 * D_MODEL), lambda s: (0, 0))],
        out_specs=[t_spec] * 2 + specs + specs,
        out_shape=[jax.ShapeDtypeStruct((batch, D_MODEL, kept), F32)] * 2 + shapes + shapes,
        scratch_shapes=[pltpu.VMEM((N_CHUNK, TT, LANES), F32)],
        compiler_params=pltpu.CompilerParams(dimension_semantics=("arbitrary",),
                                             vmem_limit_bytes=VMEM_LIMIT),
        name="kv_proj_prompt",
    )(x, w)


def _q_proj_prompt(x, w, *, batch, seq_len):
    n_prompt = batch * seq_len
    specs, shapes = _residue_specs(batch, seq_len)
    return pl.pallas_call(
        _q_prompt_kernel,
        grid=(n_prompt // TT,),
        in_specs=[pl.BlockSpec((TT, D_MODEL), lambda s: (s, 0)),
                  pl.BlockSpec((D_MODEL, N_DIL * D_MODEL), lambda s: (0, 0))],
        out_specs=specs,
        out_shape=shapes,
        scratch_shapes=[pltpu.VMEM((N_CHUNK, TT, LANES), F32)],
        compiler_params=pltpu.CompilerParams(dimension_semantics=("arbitrary",),
                                             vmem_limit_bytes=VMEM_LIMIT),
        name="q_proj_prompt",
    )(x, w)


def _proj_sample_kernel(x_ref, w_ref, o_ref, *, scale):
    o_ref[...] = jnp.dot(x_ref[...].astype(BF16), w_ref[...], preferred_element_type=F32) * scale


def _proj_sample(x, w, *, n_prompt, n_sample, scale):
    n_out = w.shape[1]
    first = n_prompt // TT
    return pl.pallas_call(
        functools.partial(_proj_sample_kernel, scale=scale),
        grid=(n_sample // TT, n_out // D_MODEL),
        in_specs=[pl.BlockSpec((TT, D_MODEL), lambda s, c: (first + s, 0)),
                  pl.BlockSpec((D_MODEL, D_MODEL), lambda s, c: (0, c))],
        out_specs=pl.BlockSpec((TT, D_MODEL), lambda s, c: (s, c)),
        out_shape=jax.ShapeDtypeStruct((n_sample, n_out), F32),
        compiler_params=pltpu.CompilerParams(dimension_semantics=("arbitrary", "arbitrary"),
                                             vmem_limit_bytes=VMEM_LIMIT),
        name="proj_sample",
    )(x, w)


def _attn_prompt_kernel(q_ref, kp_ref, kc_ref, vp_ref, vc_ref, o_ref, lse_ref, *, group):
    i = pl.program_id(2)
    dil = DIL_PATTERNS[group][1]
    a = lax.broadcasted_iota(jnp.int32, (2 * QB, 2 * QB), 0) % QB
    j = lax.broadcasted_iota(jnp.int32, (2 * QB, 2 * QB), 1)
    delta = QB + a - j
    ok = (delta >= 0) & (delta <= QB) & ((j >= QB) | (i > 0))
    dist = (delta * dil).astype(F32)
    top_rows = lax.broadcasted_iota(jnp.int32, (2 * QB, 1), 0) < QB
    lane2 = lax.broadcasted_iota(jnp.int32, (2 * QB, LANES), 1)
    row2 = lax.broadcasted_iota(jnp.int32, (2 * QB, LANES), 0)
    own_dims = (lane2 < HEAD_DIM) == (row2 < QB)
    lane = lax.broadcasted_iota(jnp.int32, (QB, LANES), 1)
    lse_blk = jnp.zeros((QB, LANES), F32)
    for hp in range(N_HEADS // 2):
        cols = pl.ds(hp * LANES, LANES)
        q2 = q_ref[:, cols]
        k2 = jnp.concatenate([kp_ref[:, cols], kc_ref[:, cols]], axis=0)
        v2 = jnp.concatenate([vp_ref[:, cols], vc_ref[:, cols]], axis=0)
        qq = jnp.concatenate([q2, q2], axis=0)
        qm = jnp.where(own_dims, qq, jnp.zeros_like(qq))
        slope = jnp.where(top_rows, float(SLOPES[group, 2 * hp]), float(SLOPES[group, 2 * hp + 1]))
        s = lax.dot_general(qm, k2, NT_DIMS, preferred_element_type=F32)
        s = jnp.where(ok, s - slope * dist, NEG)
        mx = jnp.max(s, axis=1, keepdims=True)
        pexp = jnp.exp(s - mx)
        den = jnp.sum(pexp, axis=1, keepdims=True)
        pv = jnp.dot(pexp.astype(BF16), v2, preferred_element_type=F32) / den
        lse = mx + jnp.log(den)
        o_ref[:, cols] = jnp.where(lane < HEAD_DIM, pv[0:QB], pv[QB:2 * QB]).astype(o_ref.dtype)
        lse_blk = jnp.where(lane == 2 * hp, lse[0:QB], jnp.where(lane == 2 * hp + 1, lse[QB:2 * QB], lse_blk))
    lse_ref[...] = lse_blk


def _attn_prompt(qg, kg, vg, *, group):
    batch, dil, rows, _ = qg.shape
    nq = rows // QB
    cur = lambda b, r, i: (b, r, i, 0)
    prev = lambda b, r, i: (b, r, jnp.maximum(i - 1, 0), 0)
    blk = lambda im: pl.BlockSpec((None, None, QB, D_MODEL), im)
    return pl.pallas_call(
        functools.partial(_attn_prompt_kernel, group=group),
        grid=(batch, dil, nq),
        in_specs=[blk(cur), blk(prev), blk(cur), blk(prev), blk(cur)],
        out_specs=[blk(cur), pl.BlockSpec((None, None, QB, LANES), cur)],
        out_shape=[jax.ShapeDtypeStruct((batch, dil, rows, D_MODEL), BF16),
                   jax.ShapeDtypeStruct((batch, dil, rows, LANES), F32)],
        compiler_params=pltpu.CompilerParams(dimension_semantics=("arbitrary",) * 3,
                                             vmem_limit_bytes=VMEM_LIMIT),
        name=f"attn_prompt_g{group}",
    )(qg, kg, kg, vg, vg)


def _attn_sample_kernel(q_ref, kt_ref, vt_ref, kn_ref, vn_ref, sl_ref, o_ref, *shift_refs, dec_seq):
    n_heads = q_ref.shape[0]
    nrow = N_DIL * dec_seq
    ncol = W_MAX + LANES
    new0 = LANES - dec_seq
    row = lax.broadcasted_iota(jnp.int32, (nrow, ncol), 0)
    col = lax.broadcasted_iota(jnp.int32, (nrow, ncol), 1)
    grp = row // dec_seq
    kpos = jnp.where(col < W_MAX, col, col - new0)
    dist = W_MAX + (row - grp * dec_seq) - kpos
    win = jnp.where(grp == 0, DIL_PATTERNS[0][0], jnp.where(grp == 1, DIL_PATTERNS[1][0], DIL_PATTERNS[2][0]))
    dmask = jnp.where(grp == 0, DIL_PATTERNS[0][1] - 1,
                      jnp.where(grp == 1, DIL_PATTERNS[1][1] - 1, DIL_PATTERNS[2][1] - 1))
    ok = ((col < W_MAX) | (col >= W_MAX + new0)) & (dist >= 0) & (dist <= win) & ((dist & dmask) == 0)
    distf = dist.astype(F32)
    lane = lax.broadcasted_iota(jnp.int32, (dec_seq, LANES), 1)

    def scores(h):
        q = q_ref[h].astype(BF16)
        s = jnp.concatenate([jnp.dot(q, kt_ref[h].astype(BF16), preferred_element_type=F32),
                             jnp.dot(q, kn_ref[h].astype(BF16), preferred_element_type=F32)], axis=1)
        s = jnp.where(ok, s - sl_ref[h][:, 0:1] * distf, NEG)
        mx = jnp.max(s, axis=1, keepdims=True)
        pexp = jnp.exp(s - mx)
        den = jnp.sum(pexp, axis=1, keepdims=True)
        return pexp.astype(BF16), den, mx + jnp.log(den)

    def mix_groups(o, lse):
        parts = [o[g * dec_seq:(g + 1) * dec_seq] for g in range(N_DIL)]
        ls = [lse[g * dec_seq:(g + 1) * dec_seq] for g in range(N_DIL)]
        top = jnp.maximum(jnp.maximum(ls[0], ls[1]), ls[2])
        ws = [jnp.exp(l - top) for l in ls]
        return (ws[0] * parts[0] + ws[1] * parts[1] + ws[2] * parts[2]) / (ws[0] + ws[1] + ws[2])

    for hp in range(n_heads // 2):
        vt2 = vt_ref[2 * hp:2 * hp + 2].reshape(2 * HEAD_DIM, W_MAX).astype(BF16)
        vn2 = vn_ref[2 * hp:2 * hp + 2].reshape(2 * HEAD_DIM, LANES).astype(BF16)
        mixed = []
        for half in range(2):
            pb, den, lse = scores(2 * hp + half)
            acc = (lax.dot_general(pb[:, 0:W_MAX], vt2, NT_DIMS, preferred_element_type=F32)
                   + lax.dot_general(pb[:, W_MAX:ncol], vn2, NT_DIMS, preferred_element_type=F32))
            mixed.append(mix_groups(acc / den, lse))
        o_ref[:, hp * LANES:(hp + 1) * LANES] = jnp.where(lane < HEAD_DIM, mixed[0], mixed[1])

    if shift_refs:
        lane_s = lax.broadcasted_iota(jnp.int32, (HEAD_DIM, LANES), 1)
        n_chunks = W_MAX // LANES
        for src_ref, new_ref, dst_ref in ((kt_ref, kn_ref, shift_refs[0]), (vt_ref, vn_ref, shift_refs[1])):
            for h in range(n_heads):
                rolled = [pltpu.roll(src_ref[h, :, c * LANES:(c + 1) * LANES], new0, 1) for c in range(n_chunks)]
                rolled.append(new_ref[h])
                for c in range(n_chunks):
                    dst_ref[h, :, c * LANES:(c + 1) * LANES] = jnp.where(lane_s < new0, rolled[c], rolled[c + 1])


def _attn_sample(q4, kt, vt, knp, vnp, slope4, *, dec_seq, heads_per_step, shift):
    nb = q4.shape[0]
    hps = heads_per_step
    nrow = N_DIL * dec_seq
    per_head = lambda last2: pl.BlockSpec((None, hps) + last2, lambda b, c: (b, c, 0, 0))
    out_specs = [pl.BlockSpec((None, dec_seq, hps * HEAD_DIM), lambda b, c: (b, 0, c))]
    out_shape = [jax.ShapeDtypeStruct((nb, dec_seq, D_MODEL), F32)]
    if shift:
        out_specs += [per_head((HEAD_DIM, W_MAX))] * 2
        out_shape += [jax.ShapeDtypeStruct(kt.shape, kt.dtype)] * 2
    return pl.pallas_call(
        functools.partial(_attn_sample_kernel, dec_seq=dec_seq),
        grid=(nb, N_HEADS // hps),
        in_specs=[per_head((nrow, HEAD_DIM)), per_head((HEAD_DIM, W_MAX)), per_head((HEAD_DIM, W_MAX)),
                  per_head((HEAD_DIM, LANES)), per_head((HEAD_DIM, LANES)),
                  pl.BlockSpec((hps, nrow, LANES), lambda b, c: (c, 0, 0))],
        out_specs=out_specs,
        out_shape=out_shape,
        compiler_params=pltpu.CompilerParams(dimension_semantics=("arbitrary", "arbitrary"),
                                             vmem_limit_bytes=VMEM_LIMIT),
        name="attn_sample_shift" if shift else "attn_sample",
    )(q4, kt, vt, knp, vnp, slope4)


def _attn_out_kernel(x_ref, o0_ref, o1_ref, o2_ref, l0_ref, l1_ref, l2_ref, os_ref, e_ref, wo_ref, p_ref,
                     lng_ref, lnb_ref, wrt_ref, br_ref, r_ref, cls_ref, nat_o, nat_l, *, n_prompt_steps):
    s = pl.program_id(0)

    def finish(o):
        mix = jnp.dot(o.astype(BF16), wo_ref[...], preferred_element_type=F32)
        _tail(x_ref[...], mix, p_ref[...], lng_ref[...], lnb_ref[...], wrt_ref, br_ref, r_ref, cls_ref)

    def natural(ref, scr):
        dil = ref.shape[0]
        if dil == 1:
            return ref[0].astype(F32)
        n_chunk = ref.shape[2] // LANES
        for r in range(dil):
            v = ref[r].astype(F32)
            for c in range(n_chunk):
                scr.at[c][pl.ds(r, TT // dil, stride=dil), :] = v[:, c * LANES:(c + 1) * LANES]
        return jnp.concatenate([scr[c] for c in range(n_chunk)], axis=1)

    @pl.when(s < n_prompt_steps)
    def _():
        lses = [natural(l_ref, nat_l.at[g]) for g, l_ref in enumerate((l0_ref, l1_ref, l2_ref))]
        top = jnp.maximum(jnp.maximum(lses[0], lses[1]), lses[2])
        ws = [jnp.exp(l - top) for l in lses]
        tot = ws[0] + ws[1] + ws[2]
        o = jnp.zeros((TT, D_MODEL), F32)
        for wgt, o_ref in zip(ws, (o0_ref, o1_ref, o2_ref)):
            hi, lo = _split_bf16(wgt / tot)
            wide = (jnp.dot(hi, e_ref[...], preferred_element_type=F32)
                    + jnp.dot(lo, e_ref[...], preferred_element_type=F32))
            o = o + wide * natural(o_ref, nat_o)
        finish(o)

    @pl.when(s >= n_prompt_steps)
    def _():
        finish(os_ref[...])


def _attn_out_layer(x, o3, l3, o_s, expand, wo, p_flat, lng, lnb, wrt, br, *, n_prompt, seq_len):
    ntok = p_flat.shape[0]
    npst = n_prompt // TT
    sps = seq_len // TT
    blk = lambda s: (s, 0)
    sblk = lambda s: (jnp.maximum(s - npst, 0), 0)
    const = lambda s: (0, 0)

    def res_spec(dil, width):
        def im(s):
            sp = jnp.minimum(s, npst - 1)
            return (sp // sps, 0, sp % sps, 0)
        return pl.BlockSpec((None, dil, TT // dil, width), im)

    return pl.pallas_call(
        functools.partial(_attn_out_kernel, n_prompt_steps=npst),
        grid=(ntok // TT,),
        in_specs=[pl.BlockSpec((TT, D_MODEL), blk)]
        + [res_spec(dil, D_MODEL) for _, dil in DIL_PATTERNS]
        + [res_spec(dil, LANES) for _, dil in DIL_PATTERNS]
        + [pl.BlockSpec((TT, D_MODEL), sblk),
           pl.BlockSpec((LANES, D_MODEL), const),
           pl.BlockSpec((D_MODEL, D_MODEL), const),
           pl.BlockSpec((TT, D_PLE), blk),
           pl.BlockSpec((1, D_MODEL), const),
           pl.BlockSpec((1, D_MODEL), const),
           pl.BlockSpec((N_EXPERTS, D_MODEL), const),
           pl.BlockSpec((N_EXPERTS, 1), const)],
        out_specs=[pl.BlockSpec((TT, ROW_W), blk), pl.BlockSpec((1, TT), lambda s: (0, s))],
        out_shape=[jax.ShapeDtypeStruct((ntok, ROW_W), F32), jax.ShapeDtypeStruct((1, ntok), jnp.int32)],
        scratch_shapes=[pltpu.VMEM((N_CHUNK, TT, LANES), F32), pltpu.VMEM((N_DIL, 1, TT, LANES), F32)],
        compiler_params=pltpu.CompilerParams(dimension_semantics=("arbitrary",),
                                             vmem_limit_bytes=VMEM_LIMIT),
        name="attn_out",
    )(x, *o3, *l3, o_s, expand, wo, p_flat, lng, lnb, wrt, br)


def kernel(x_prompt, x_sample, state_pool, cache_k, cache_v, p_prompt, p_sample, w_pool, pool_scale, w_kv,
           w_q, w_o, ln_g, ln_b, w_router, b_router, w_exp_gate, w_exp_up, w_exp_down, w_ple_gate,
           w_ple_proj):
    batch, seq_len, d = x_prompt.shape
    nb, dec_seq, _ = x_sample.shape
    wbuf = cache_k.shape[1]
    n_prompt = batch * seq_len
    n_sample = nb * dec_seq
    ntok = n_prompt + n_sample
    assert d == D_MODEL and wbuf == W_MAX and dec_seq == 8
    assert seq_len % (QB * DIL_PATTERNS[2][1]) == 0 and n_sample % TT == 0 and seq_len % TT == 0

    wrt2 = w_router.T.astype(BF16)
    br = b_router.astype(F32).reshape(N_EXPERTS, 1)
    wg, wu, wd = w_exp_gate.astype(BF16), w_exp_up.astype(BF16), w_exp_down.astype(BF16)
    wpg, wpp = w_ple_gate.astype(BF16), w_ple_proj.astype(BF16)
    wp = w_pool.astype(BF16)
    head_of_col = np.arange(D_MODEL) // HEAD_DIM
    expand = jnp.asarray(np.arange(LANES)[:, None] == head_of_col[None, :], BF16)
    slope4 = jnp.asarray(np.ascontiguousarray(np.broadcast_to(
        np.repeat(SLOPES.T, dec_seq, axis=1)[:, :, None], (N_HEADS, N_DIL * dec_seq, LANES))))

    p_flat = jnp.concatenate([p_prompt.reshape(DEPTH, n_prompt, D_PLE),
                              p_sample.reshape(DEPTH, n_sample, D_PLE)], axis=1)

    def moe_stage(rows, cls, layer):
        gidx, sidx, elo, ehi, nused = _route_plan(cls.reshape(ntok))
        return _moe_layer(rows, gidx, sidx, elo, ehi, nused, wg, wu, wd, wpg[layer], wpp[layer],
                          ln_g[layer, 1:3], ln_b[layer, 1:3], layer=layer)

    def to_heads_minor(a, width):
        a = a.reshape(nb, dec_seq, N_HEADS, HEAD_DIM).transpose(0, 2, 3, 1)
        return jnp.pad(a, ((0, 0), (0, 0), (0, 0), (width - dec_seq, 0)))

    x = x_prompt.reshape(n_prompt, D_MODEL)
    xs = x_sample
    pool_p, pool_s = [], []
    for i in range(N_A):
        ext_s = jnp.concatenate([jnp.zeros((nb, HALO - POOL_HIST, D_MODEL), F32), state_pool[i], xs], axis=1)
        pool_p.append(jnp.stack([x[(b + 1) * seq_len - POOL_HIST:(b + 1) * seq_len] for b in range(batch)], 0))
        pool_s.append(ext_s[:, HALO + dec_seq - POOL_HIST:])
        rows, cls = _pool_layer(x, ext_s, p_flat[i], wp[i], pool_scale[i].reshape(1, D_MODEL),
                                ln_g[i, 0].reshape(1, D_MODEL), ln_b[i, 0].reshape(1, D_MODEL), wrt2, br,
                                n_prompt=n_prompt, seq_len=seq_len)
        x = moe_stage(rows, cls, i)
        xs = x[n_prompt:ntok].reshape(nb, dec_seq, D_MODEL)

    w_kv_b = w_kv.astype(BF16)
    kt_p, vt_p, *kv_res = _kv_proj_prompt(x, w_kv_b, batch=batch, seq_len=seq_len)
    k_res, v_res = kv_res[:N_DIL], kv_res[N_DIL:]
    kv_s = _proj_sample(x, w_kv_b, n_prompt=n_prompt, n_sample=n_sample, scale=1.0)
    knp = to_heads_minor(kv_s[:, :D_MODEL], LANES)
    vnp = to_heads_minor(kv_s[:, D_MODEL:], LANES)
    kt = cache_k.transpose(0, 2, 3, 1)
    vt = cache_v.transpose(0, 2, 3, 1)

    for jl in range(N_B):
        i = N_A + jl
        w_q_b = w_q[jl].astype(BF16)
        q_res = _q_proj_prompt(x, w_q_b, batch=batch, seq_len=seq_len)
        o3, l3 = [], []
        for g in range(N_DIL):
            o_g, l_g = _attn_prompt(q_res[g], k_res[g], v_res[g], group=g)
            o3.append(o_g)
            l3.append(l_g)
        q_s = _proj_sample(x, w_q_b, n_prompt=n_prompt, n_sample=n_sample, scale=HEAD_DIM ** -0.5)
        q4 = (q_s.reshape(nb, dec_seq, N_DIL, N_HEADS, HEAD_DIM)
              .transpose(0, 3, 2, 1, 4).reshape(nb, N_HEADS, N_DIL * dec_seq, HEAD_DIM))
        if jl == 0:
            o_s, kt_new, vt_new = _attn_sample(q4, kt, vt, knp, vnp, slope4, dec_seq=dec_seq,
                                               heads_per_step=4, shift=True)
        else:
            (o_s,) = _attn_sample(q4, kt, vt, knp, vnp, slope4, dec_seq=dec_seq, heads_per_step=8, shift=False)
        rows, cls = _attn_out_layer(x, o3, l3, o_s.reshape(n_sample, D_MODEL), expand, w_o[jl].astype(BF16),
                                    p_flat[i], ln_g[i, 0].reshape(1, D_MODEL), ln_b[i, 0].reshape(1, D_MODEL),
                                    wrt2, br, n_prompt=n_prompt, seq_len=seq_len)
        x = moe_stage(rows, cls, i)

    kept = kt_p.shape[2]
    k_prompt = kt_p.reshape(batch, N_HEADS, HEAD_DIM, kept).transpose(0, 3, 1, 2)
    v_prompt = vt_p.reshape(batch, N_HEADS, HEAD_DIM, kept).transpose(0, 3, 1, 2)
    return (x[:n_prompt].reshape(batch, seq_len, D_MODEL),
            x[n_prompt:ntok].reshape(nb, dec_seq, D_MODEL),
            jnp.stack(pool_p, 0), jnp.stack(pool_s, 0),
            k_prompt, v_prompt,
            kt_new.transpose(0, 3, 1, 2), vt_new.transpose(0, 3, 1, 2))
```

```python
import functools

import numpy as np
import jax
import jax.numpy as jnp
from jax import lax
from jax.experimental import pallas as pl
from jax.experimental.pallas import tpu as pltpu

D_MODEL = 1024
DEPTH = 4
N_A = DEPTH // 2
N_B = DEPTH - N_A
POOL_WINDOWS = (2, 4, 8, 16)
POOL_GROUP = D_MODEL // len(POOL_WINDOWS)
POOL_HIST = max(POOL_WINDOWS) - 1
HEAD_DIM = 64
N_HEADS = D_MODEL // HEAD_DIM
DIL_PATTERNS = ((128, 1), (512, 4), (2048, 16))
N_DIL = len(DIL_PATTERNS)
W_MAX = 2048
N_EXPERTS = 16
N_EXPERT_GROUPS = 4
EXPERTS_PER_GROUP = 4
D_EXPERT = 512
D_PLE = 256
ALPHA = (2 * DEPTH) ** 0.25
LN_EPS = 1e-5

PAIRS = ((0, 1), (0, 2), (0, 3), (1, 2), (1, 3), (2, 3))
N_CLASSES = N_EXPERT_GROUPS * len(PAIRS)

LANES = 128
HALO = 16
TT = 512
TM = 256
DMA_UNROLL = 8
assert D_MODEL == 8 * LANES
QB = 128
ROW_W = D_MODEL + D_PLE + LANES
VMEM_LIMIT = 48 * 1024 * 1024
NEG = -1e30

F32 = jnp.float32
BF16 = jnp.bfloat16
NT_DIMS = (((1,), (1,)), ((), ()))


def _alibi_slopes():
    n = N_DIL * N_HEADS
    return (2.0 ** (-8.0 * np.arange(1, n + 1) / n)).astype(np.float32).reshape(N_DIL, N_HEADS)


SLOPES = _alibi_slopes()


def _ln(x, g, b):
    mu = jnp.mean(x, axis=-1, keepdims=True)
    xc = x - mu
    var = jnp.mean(xc * xc, axis=-1, keepdims=True)
    return xc * lax.rsqrt(var + LN_EPS) * g + b


def _split_bf16(x):
    hi = x.astype(BF16)
    lo = (x - hi.astype(F32)).astype(BF16)
    return hi, lo


def _route(x1, wrt_ref, br_ref):
    z = lax.dot_general(wrt_ref[...], x1.astype(BF16), NT_DIMS, preferred_element_type=F32)
    s = jax.nn.sigmoid(z)
    sel = s + br_ref[...]
    a = [sel[k:k + 1, :] for k in range(N_EXPERTS)]
    sv = [s[k:k + 1, :] for k in range(N_EXPERTS)]
    one = jnp.ones_like(a[0])
    zero = jnp.zeros_like(a[0])
    gscore, chosen = [], []
    for g in range(N_EXPERT_GROUPS):
        ag = a[4 * g:4 * g + 4]
        cnt = []
        for j in range(4):
            c = zero
            for k in range(4):
                if k == j:
                    continue
                beats = (ag[k] > ag[j]) | (ag[k] == ag[j]) if k < j else (ag[k] > ag[j])
                c = c + jnp.where(beats, one, zero)
            cnt.append(c)
        top = zero
        sec = zero
        for j in range(4):
            top = top + jnp.where(cnt[j] == 0.0, ag[j], zero)
            sec = sec + jnp.where(cnt[j] == 1.0, ag[j], zero)
        gscore.append(top + sec)
        chosen.append([jnp.where(cnt[j] < 2.0, one, zero) for j in range(4)])
    best = gscore[0]
    gi = zero
    for g in range(1, N_EXPERT_GROUPS):
        better = gscore[g] > best
        gi = jnp.where(better, float(g), gi)
        best = jnp.where(better, gscore[g], best)
    m = []
    sg = []
    for j in range(4):
        mj = chosen[0][j]
        sj = sv[j]
        for g in range(1, N_EXPERT_GROUPS):
            isg = gi == float(g)
            mj = jnp.where(isg, chosen[g][j], mj)
            sj = jnp.where(isg, sv[4 * g + j], sj)
        m.append(mj)
        sg.append(sj)
    pidx = zero
    s_lo = zero
    s_hi = zero
    for idx, (p0, p1) in enumerate(PAIRS):
        hit = (m[p0] * m[p1]) > 0.5
        pidx = jnp.where(hit, float(idx), pidx)
        s_lo = jnp.where(hit, sg[p0], s_lo)
        s_hi = jnp.where(hit, sg[p1], s_hi)
    den = s_lo + s_hi
    cls = (gi * float(len(PAIRS)) + pidx).astype(jnp.int32)
    return s_lo / den, s_hi / den, cls


def _tail(x, mix, p, lng, lnb, wrt_ref, br_ref, r_ref, cls_ref):
    x1 = _ln(ALPHA * x + mix, lng, lnb)
    g_lo, g_hi, cls = _route(x1, wrt_ref, br_ref)
    n = x.shape[0]
    row = lax.broadcasted_iota(jnp.int32, (LANES, n), 0)
    meta_t = jnp.where(row == 0, g_lo, jnp.where(row == 1, g_hi, 0.0))
    r_ref[:, 0:D_MODEL] = x1
    r_ref[:, D_MODEL:D_MODEL + D_PLE] = p
    r_ref[:, D_MODEL + D_PLE:ROW_W] = meta_t.T
    cls_ref[...] = cls


def _pool_kernel(x_ref, halo_ref, exts_ref, p_ref, wp_ref, sc_ref, lng_ref, lnb_ref, wrt_ref, br_ref,
                 r_ref, cls_ref, ext_scr, *, n_prompt_steps, steps_per_seq):
    s = pl.program_id(0)

    def finish(x, diff_parts, p):
        ys = [jnp.dot(diff_parts[g].astype(BF16), wp_ref[g], preferred_element_type=F32)
              for g in range(len(POOL_WINDOWS))]
        mix = jnp.concatenate(ys, axis=1) * sc_ref[...]
        _tail(x, mix, p, lng_ref[...], lnb_ref[...], wrt_ref, br_ref, r_ref, cls_ref)

    @pl.when(s < n_prompt_steps)
    def _():
        t = s % steps_per_seq
        keep = jnp.where(t > 0, 1.0, 0.0)
        ext_scr[0:HALO, :] = halo_ref[...] * keep
        ext_scr[HALO:HALO + TT, :] = x_ref[...]
        tpos = (t * TT + lax.broadcasted_iota(jnp.int32, (TT, 1), 0)).astype(F32)
        parts = []
        for g, w in enumerate(POOL_WINDOWS):
            cols = pl.ds(g * POOL_GROUP, POOL_GROUP)
            xg = ext_scr[pl.ds(HALO, TT), cols]
            acc = xg
            for k in range(1, w):
                acc = acc + ext_scr[pl.ds(HALO - k, TT), cols]
            cnt = jnp.minimum(float(w), tpos + 1.0)
            parts.append(acc / cnt - xg)
        finish(x_ref[...], parts, p_ref[...])

    @pl.when(s >= n_prompt_steps)
    def _():
        nb = exts_ref.shape[0]
        hist = exts_ref.shape[1] - HALO
        parts = []
        for g, w in enumerate(POOL_WINDOWS):
            cols = pl.ds(g * POOL_GROUP, POOL_GROUP)
            xg = exts_ref[:, pl.ds(HALO, hist), cols]
            acc = xg
            for k in range(1, w):
                acc = acc + exts_ref[:, pl.ds(HALO - k, hist), cols]
            parts.append((acc / float(w) - xg).reshape(nb * hist, POOL_GROUP))
        x = exts_ref[:, pl.ds(HALO, hist), :].reshape(nb * hist, D_MODEL)
        finish(x, parts, p_ref[...])


def _pool_layer(xp_flat, ext_s, p_flat, wp, sc, lng, lnb, wrt, br, *, n_prompt, seq_len):
    ntok = p_flat.shape[0]
    n_steps = ntok // TT
    npst = n_prompt // TT
    sps = seq_len // TT
    dec_seq = ext_s.shape[1] - HALO
    nb = TT // dec_seq
    kern = functools.partial(_pool_kernel, n_prompt_steps=npst, steps_per_seq=sps)
    const = lambda s: (0, 0)
    return pl.pallas_call(
        kern,
        grid=(n_steps,),
        in_specs=[
            pl.BlockSpec((TT, D_MODEL), lambda s: (jnp.minimum(s, npst - 1), 0)),
            pl.BlockSpec((HALO, D_MODEL),
                         lambda s: (jnp.maximum(jnp.minimum(s, npst - 1) * (TT // HALO) - 1, 0), 0)),
            pl.BlockSpec((nb, HALO + dec_seq, D_MODEL), lambda s: (jnp.maximum(s - npst, 0), 0, 0)),
            pl.BlockSpec((TT, D_PLE), lambda s: (s, 0)),
            pl.BlockSpec((len(POOL_WINDOWS), POOL_GROUP, POOL_GROUP), lambda s: (0, 0, 0)),
            pl.BlockSpec((1, D_MODEL), const),
            pl.BlockSpec((1, D_MODEL), const),
            pl.BlockSpec((1, D_MODEL), const),
            pl.BlockSpec((N_EXPERTS, D_MODEL), const),
            pl.BlockSpec((N_EXPERTS, 1), const),
        ],
        out_specs=[
            pl.BlockSpec((TT, ROW_W), lambda s: (s, 0)),
            pl.BlockSpec((1, TT), lambda s: (0, s)),
        ],
        out_shape=[
            jax.ShapeDtypeStruct((ntok, ROW_W), F32),
            jax.ShapeDtypeStruct((1, ntok), jnp.int32),
        ],
        scratch_shapes=[pltpu.VMEM((HALO + TT, D_MODEL), F32)],
        compiler_params=pltpu.CompilerParams(dimension_semantics=("arbitrary",),
                                             vmem_limit_bytes=VMEM_LIMIT),
        name="pool_mixer",
    )(xp_flat, xp_flat, ext_s, p_flat, wp, sc, lng, lnb, wrt, br)


def _moe_kernel(tok_ref, elo_ref, ehi_ref, nrows_ref, r_hbm, wg_lo, wu_lo, wd_lo, wg_hi, wu_hi, wd_hi,
                wpg_ref, wpp_ref, lng_ref, lnb_ref, x_hbm, rbuf, ybuf, gsem, ssem):
    k = pl.program_id(0)
    n_tiles = pl.num_programs(0)
    slot = k % 2

    def gather_start(tile, s):
        def body(j, c):
            r = tok_ref[tile * TM + j]
            pltpu.make_async_copy(r_hbm.at[pl.ds(r, 1)], rbuf.at[s, pl.ds(j, 1)], gsem.at[s]).start()
            return c
        lax.fori_loop(0, TM, body, 0, unroll=DMA_UNROLL)

    def gather_wait(s):
        pltpu.make_async_copy(r_hbm.at[pl.ds(0, TM)], rbuf.at[s], gsem.at[s]).wait()

    def scatter_row(tile, s, j):
        d = tok_ref[tile * TM + j]
        pltpu.make_async_copy(ybuf.at[s, pl.ds(j, 1)], x_hbm.at[pl.ds(d, 1)], ssem.at[s]).start()

    def scatter_start(tile, s):
        n = nrows_ref[tile]

        def group(g, c):
            for u in range(DMA_UNROLL):
                scatter_row(tile, s, g * DMA_UNROLL + u)
            return c

        def single(j, c):
            scatter_row(tile, s, j)
            return c

        lax.fori_loop(0, n // DMA_UNROLL, group, 0)
        lax.fori_loop((n // DMA_UNROLL) * DMA_UNROLL, n, single, 0)

    def scatter_wait(tile, s):
        n = nrows_ref[tile]
        for bit in [1 << b for b in range(TM.bit_length())]:
            if bit >= 8:
                desc = pltpu.make_async_copy(ybuf.at[s, pl.ds(0, bit)], x_hbm.at[pl.ds(0, bit)], ssem.at[s])
            else:
                desc = pltpu.make_async_copy(ybuf.at[s, pl.ds(0, 8), pl.ds(0, bit * LANES)],
                                             x_hbm.at[pl.ds(0, 8), pl.ds(0, bit * LANES)], ssem.at[s])

            @pl.when((n & bit) != 0)
            def _():
                desc.wait()

    next_valid = (k + 1 < n_tiles) & (nrows_ref[jnp.minimum(k + 1, n_tiles - 1)] != 0)

    @pl.when(k == 0)
    def _():
        gather_start(0, 0)

    @pl.when(nrows_ref[k] != 0)
    def _():
        @pl.when(next_valid)
        def _():
            gather_start(k + 1, 1 - slot)

        gather_wait(slot)
        x1 = rbuf[slot, :, 0:D_MODEL]
        p = rbuf[slot, :, D_MODEL:D_MODEL + D_PLE]
        gates = rbuf[slot, :, D_MODEL + D_PLE:ROW_W]
        xb = x1.astype(BF16)

        def expert(wg, wu, wd):
            hg = jnp.dot(xb, wg[...], preferred_element_type=F32)
            hu = jnp.dot(xb, wu[...], preferred_element_type=F32)
            h = (hg * jax.nn.sigmoid(hg)) * hu
            return jnp.dot(h.astype(BF16), wd[...], preferred_element_type=F32)

        moe = gates[:, 0:1] * expert(wg_lo, wu_lo, wd_lo)
        moe = moe + gates[:, 1:2] * expert(wg_hi, wu_hi, wd_hi)
        x2 = _ln(ALPHA * x1 + moe, lng_ref[0:1, :], lnb_ref[0:1, :])
        gate = jax.nn.sigmoid(jnp.dot(x2.astype(BF16), wpg_ref[...], preferred_element_type=F32))
        proj = jnp.dot(p.astype(BF16), wpp_ref[...], preferred_element_type=F32)
        ybuf[slot] = _ln(ALPHA * x2 + gate * proj, lng_ref[1:2, :], lnb_ref[1:2, :])

        @pl.when(k > 0)
        def _():
            scatter_wait(k - 1, 1 - slot)

        scatter_start(k, slot)

        @pl.when(jnp.logical_not(next_valid))
        def _():
            scatter_wait(k, slot)


def _moe_layer(rows, slot_tok, elo, ehi, nrows, wg, wu, wd, wpg, wpp, lng2, lnb2, *, layer):
    ntok = rows.shape[0]
    n_tiles = nrows.shape[0]
    lo = lambda k, tok, elo, ehi, nrows: (layer, elo[k], 0, 0)
    hi = lambda k, tok, elo, ehi, nrows: (layer, ehi[k], 0, 0)
    const2 = lambda k, tok, elo, ehi, nrows: (0, 0)
    gu_spec = lambda im: pl.BlockSpec((None, None, D_MODEL, D_EXPERT), im)
    dn_spec = lambda im: pl.BlockSpec((None, None, D_EXPERT, D_MODEL), im)
    return pl.pallas_call(
        _moe_kernel,
        grid_spec=pltpu.PrefetchScalarGridSpec(
            num_scalar_prefetch=4,
            grid=(n_tiles,),
            in_specs=[
                pl.BlockSpec(memory_space=pl.ANY),
                gu_spec(lo), gu_spec(lo), dn_spec(lo),
                gu_spec(hi), gu_spec(hi), dn_spec(hi),
                pl.BlockSpec((D_MODEL, D_MODEL), const2),
                pl.BlockSpec((D_PLE, D_MODEL), const2),
                pl.BlockSpec((2, D_MODEL), const2),
                pl.BlockSpec((2, D_MODEL), const2),
            ],
            out_specs=pl.BlockSpec(memory_space=pl.ANY),
            scratch_shapes=[pltpu.VMEM((2, TM, ROW_W), F32), pltpu.VMEM((2, TM, D_MODEL), F32),
                            pltpu.SemaphoreType.DMA((2,)), pltpu.SemaphoreType.DMA((2,))],
        ),
        out_shape=jax.ShapeDtypeStruct((ntok, D_MODEL), F32),
        compiler_params=pltpu.CompilerParams(dimension_semantics=("arbitrary",),
                                             vmem_limit_bytes=VMEM_LIMIT),
        name="moe_ple",
    )(slot_tok, elo, ehi, nrows, rows, wg, wu, wd, wg, wu, wd, wpg, wpp, lng2, lnb2)


def _route_plan(cls):
    ntok = cls.shape[0]
    n_tiles = -(-(ntok + N_CLASSES * (TM - 1)) // TM)
    ns = n_tiles * TM
    onehot = (cls[:, None] == jnp.arange(N_CLASSES, dtype=jnp.int32)[None, :]).astype(jnp.int32)
    csum = jnp.cumsum(onehot, axis=0)
    rank = jnp.sum(csum * onehot, axis=1) - 1
    counts = csum[-1]
    padded = ((counts + TM - 1) // TM) * TM
    ends = jnp.cumsum(padded)
    starts = ends - padded
    pos = jnp.sum(onehot * starts[None, :], axis=1) + rank
    tok = jnp.arange(ntok, dtype=jnp.int32)
    src = jnp.zeros((ns,), jnp.int32).at[pos].set(tok)
    tile_start = jnp.arange(n_tiles, dtype=jnp.int32) * TM
    tile_cls = jnp.sum((tile_start[:, None] >= ends[None, :]).astype(jnp.int32), axis=1)
    last_cls = jnp.max(jnp.where(counts > 0, jnp.arange(N_CLASSES, dtype=jnp.int32), 0))
    tile_cls = jnp.minimum(tile_cls, last_cls)
    nrows = jnp.clip((starts + counts)[tile_cls] - tile_start, 0, TM)
    grp = tile_cls // len(PAIRS)
    pidx = tile_cls % len(PAIRS)
    p0 = jnp.asarray([p[0] for p in PAIRS], jnp.int32)[pidx]
    p1 = jnp.asarray([p[1] for p in PAIRS], jnp.int32)[pidx]
    return src, grp * EXPERTS_PER_GROUP + p0, grp * EXPERTS_PER_GROUP + p1, nrows


N_CHUNK = D_MODEL // LANES


def _store_residues(val, scr, out_refs, dils):
    if any(d > 1 for d in dils):
        for c in range(N_CHUNK):
            scr[c] = val[:, c * LANES:(c + 1) * LANES]
    for dil, ref in zip(dils, out_refs):
        if dil == 1:
            ref[0] = val.astype(ref.dtype)
            continue
        for r in range(dil):
            for c in range(N_CHUNK):
                ref[r, :, c * LANES:(c + 1) * LANES] = (
                    scr.at[c][pl.ds(r, TT // dil, stride=dil), :].astype(ref.dtype))


ALL_DILS = tuple(d for _, d in DIL_PATTERNS)


def _kv_prompt_kernel(x_ref, w_ref, kt_ref, vt_ref, k0_ref, k1_ref, k2_ref, v0_ref, v1_ref, v2_ref, scr,
                      *, steps_per_seq, first_kept):
    kv = jnp.dot(x_ref[...].astype(BF16), w_ref[...], preferred_element_type=F32)
    k = kv[:, 0:D_MODEL]
    v = kv[:, D_MODEL:2 * D_MODEL]

    @pl.when(pl.program_id(0) % steps_per_seq >= first_kept)
    def _():
        kt_ref[...] = k.T
        vt_ref[...] = v.T

    _store_residues(k, scr, (k0_ref, k1_ref, k2_ref), ALL_DILS)
    _store_residues(v, scr, (v0_ref, v1_ref, v2_ref), ALL_DILS)


def _q_prompt_kernel(x_ref, w_ref, q0_ref, q1_ref, q2_ref, scr):
    xb = x_ref[...].astype(BF16)
    for g, ref in enumerate((q0_ref, q1_ref, q2_ref)):
        q = jnp.dot(xb, w_ref[:, g * D_MODEL:(g + 1) * D_MODEL], preferred_element_type=F32)
        _store_residues(q * (HEAD_DIM ** -0.5), scr, (ref,), (ALL_DILS[g],))


def _residue_specs(batch, seq_len):
    sps = seq_len // TT
    specs, shapes = [], []
    for _, dil in DIL_PATTERNS:
        specs.append(pl.BlockSpec((None, dil, TT // dil, D_MODEL), lambda s: (s // sps, 0, s % sps, 0)))
        shapes.append(jax.ShapeDtypeStruct((batch, dil, seq_len // dil, D_MODEL), BF16))
    return specs, shapes


def _kv_proj_prompt(x, w, *, batch, seq_len):
    n_prompt = batch * seq_len
    sps = seq_len // TT
    kept = min(W_MAX, seq_len)
    first_kept = (seq_len - kept) // TT
    specs, shapes = _residue_specs(batch, seq_len)
    t_spec = pl.BlockSpec((None, D_MODEL, TT), lambda s: (s // sps, 0, jnp.maximum(s % sps - first_kept, 0)))
    return pl.pallas_call(
        functools.partial(_kv_prompt_kernel, steps_per_seq=sps, first_kept=first_kept),
        grid=(n_prompt // TT,),
        in_specs=[pl.BlockSpec((TT, D_MODEL), lambda s: (s, 0)),
                  pl.BlockSpec((D_MODEL, 2 * D_MODEL), lambda s: (0, 0))],
        out_specs=[t_spec] * 2 + specs + specs,
        out_shape=[jax.ShapeDtypeStruct((batch, D_MODEL, kept), F32)] * 2 + shapes + shapes,
        scratch_shapes=[pltpu.VMEM((N_CHUNK, TT, LANES), F32)],
        compiler_params=pltpu.CompilerParams(dimension_semantics=("arbitrary",),
                                             vmem_limit_bytes=VMEM_LIMIT),
        name="kv_proj_prompt",
    )(x, w)


def _q_proj_prompt(x, w, *, batch, seq_len):
    n_prompt = batch * seq_len
    specs, shapes = _residue_specs(batch, seq_len)
    return pl.pallas_call(
        _q_prompt_kernel,
        grid=(n_prompt // TT,),
        in_specs=[pl.BlockSpec((TT, D_MODEL), lambda s: (s, 0)),
                  pl.BlockSpec((D_MODEL, N_DIL * D_MODEL), lambda s: (0, 0))],
        out_specs=specs,
        out_shape=shapes,
        scratch_shapes=[pltpu.VMEM((N_CHUNK, TT, LANES), F32)],
        compiler_params=pltpu.CompilerParams(dimension_semantics=("arbitrary",),
                                             vmem_limit_bytes=VMEM_LIMIT),
        name="q_proj_prompt",
    )(x, w)


def _proj_sample_kernel(x_ref, w_ref, o_ref, *, scale):
    o_ref[...] = jnp.dot(x_ref[...].astype(BF16), w_ref[...], preferred_element_type=F32) * scale


def _proj_sample(x, w, *, n_prompt, n_sample, scale):
    n_out = w.shape[1]
    first = n_prompt // TT
    return pl.pallas_call(
        functools.partial(_proj_sample_kernel, scale=scale),
        grid=(n_sample // TT, n_out // D_MODEL),
        in_specs=[pl.BlockSpec((TT, D_MODEL), lambda s, c: (first + s, 0)),
                  pl.BlockSpec((D_MODEL, D_MODEL), lambda s, c: (0, c))],
        out_specs=pl.BlockSpec((TT, D_MODEL), lambda s, c: (s, c)),
        out_shape=jax.ShapeDtypeStruct((n_sample, n_out), F32),
        compiler_params=pltpu.CompilerParams(dimension_semantics=("arbitrary", "arbitrary"),
                                             vmem_limit_bytes=VMEM_LIMIT),
        name="proj_sample",
    )(x, w)


def _attn_prompt_kernel(q_ref, kp_ref, kc_ref, vp_ref, vc_ref, o_ref, lse_ref, *, group):
    i = pl.program_id(2)
    dil = DIL_PATTERNS[group][1]
    a = lax.broadcasted_iota(jnp.int32, (2 * QB, 2 * QB), 0) % QB
    j = lax.broadcasted_iota(jnp.int32, (2 * QB, 2 * QB), 1)
    delta = QB + a - j
    ok = (delta >= 0) & (delta <= QB) & ((j >= QB) | (i > 0))
    dist = (delta * dil).astype(F32)
    top_rows = lax.broadcasted_iota(jnp.int32, (2 * QB, 1), 0) < QB
    lane2 = lax.broadcasted_iota(jnp.int32, (2 * QB, LANES), 1)
    row2 = lax.broadcasted_iota(jnp.int32, (2 * QB, LANES), 0)
    own_dims = (lane2 < HEAD_DIM) == (row2 < QB)
    lane = lax.broadcasted_iota(jnp.int32, (QB, LANES), 1)
    lse_blk = jnp.zeros((QB, LANES), F32)
    for hp in range(N_HEADS // 2):
        cols = pl.ds(hp * LANES, LANES)
        q2 = q_ref[:, cols]
        k2 = jnp.concatenate([kp_ref[:, cols], kc_ref[:, cols]], axis=0)
        v2 = jnp.concatenate([vp_ref[:, cols], vc_ref[:, cols]], axis=0)
        qq = jnp.concatenate([q2, q2], axis=0)
        qm = jnp.where(own_dims, qq, jnp.zeros_like(qq))
        slope = jnp.where(top_rows, float(SLOPES[group, 2 * hp]), float(SLOPES[group, 2 * hp + 1]))
        s = lax.dot_general(qm, k2, NT_DIMS, preferred_element_type=F32)
        s = jnp.where(ok, s - slope * dist, NEG)
        mx = jnp.max(s, axis=1, keepdims=True)
        pexp = jnp.exp(s - mx)
        den = jnp.sum(pexp, axis=1, keepdims=True)
        pv = jnp.dot(pexp.astype(BF16), v2, preferred_element_type=F32) / den
        lse = mx + jnp.log(den)
        o_ref[:, cols] = jnp.where(lane < HEAD_DIM, pv[0:QB], pv[QB:2 * QB]).astype(o_ref.dtype)
        lse_blk = jnp.where(lane == 2 * hp, lse[0:QB], jnp.where(lane == 2 * hp + 1, lse[QB:2 * QB], lse_blk))
    lse_ref[...] = lse_blk


def _attn_prompt(qg, kg, vg, *, group):
    batch, dil, rows, _ = qg.shape
    nq = rows // QB
    cur = lambda b, r, i: (b, r, i, 0)
    prev = lambda b, r, i: (b, r, jnp.maximum(i - 1, 0), 0)
    blk = lambda im: pl.BlockSpec((None, None, QB, D_MODEL), im)
    return pl.pallas_call(
        functools.partial(_attn_prompt_kernel, group=group),
        grid=(batch, dil, nq),
        in_specs=[blk(cur), blk(prev), blk(cur), blk(prev), blk(cur)],
        out_specs=[blk(cur), pl.BlockSpec((None, None, QB, LANES), cur)],
        out_shape=[jax.ShapeDtypeStruct((batch, dil, rows, D_MODEL), BF16),
                   jax.ShapeDtypeStruct((batch, dil, rows, LANES), F32)],
        compiler_params=pltpu.CompilerParams(dimension_semantics=("arbitrary",) * 3,
                                             vmem_limit_bytes=VMEM_LIMIT),
        name=f"attn_prompt_g{group}",
    )(qg, kg, kg, vg, vg)


def _attn_sample_kernel(q_ref, kt_ref, vt_ref, kn_ref, vn_ref, sl_ref, o_ref, *shift_refs, dec_seq):
    n_heads = q_ref.shape[0]
    nrow = N_DIL * dec_seq
    ncol = W_MAX + LANES
    new0 = LANES - dec_seq
    row = lax.broadcasted_iota(jnp.int32, (nrow, ncol), 0)
    col = lax.broadcasted_iota(jnp.int32, (nrow, ncol), 1)
    grp = row // dec_seq
    kpos = jnp.where(col < W_MAX, col, col - new0)
    dist = W_MAX + (row - grp * dec_seq) - kpos
    win = jnp.where(grp == 0, DIL_PATTERNS[0][0], jnp.where(grp == 1, DIL_PATTERNS[1][0], DIL_PATTERNS[2][0]))
    dmask = jnp.where(grp == 0, DIL_PATTERNS[0][1] - 1,
                      jnp.where(grp == 1, DIL_PATTERNS[1][1] - 1, DIL_PATTERNS[2][1] - 1))
    ok = ((col < W_MAX) | (col >= W_MAX + new0)) & (dist >= 0) & (dist <= win) & ((dist & dmask) == 0)
    distf = dist.astype(F32)
    lane = lax.broadcasted_iota(jnp.int32, (dec_seq, LANES), 1)

    def scores(h):
        q = q_ref[h].astype(BF16)
        s = jnp.concatenate([jnp.dot(q, kt_ref[h].astype(BF16), preferred_element_type=F32),
                             jnp.dot(q, kn_ref[h].astype(BF16), preferred_element_type=F32)], axis=1)
        s = jnp.where(ok, s - sl_ref[h][:, 0:1] * distf, NEG)
        mx = jnp.max(s, axis=1, keepdims=True)
        pexp = jnp.exp(s - mx)
        den = jnp.sum(pexp, axis=1, keepdims=True)
        return pexp.astype(BF16), den, mx + jnp.log(den)

    def mix_groups(o, lse):
        parts = [o[g * dec_seq:(g + 1) * dec_seq] for g in range(N_DIL)]
        ls = [lse[g * dec_seq:(g + 1) * dec_seq] for g in range(N_DIL)]
        top = jnp.maximum(jnp.maximum(ls[0], ls[1]), ls[2])
        ws = [jnp.exp(l - top) for l in ls]
        return (ws[0] * parts[0] + ws[1] * parts[1] + ws[2] * parts[2]) / (ws[0] + ws[1] + ws[2])

    for hp in range(n_heads // 2):
        vt2 = vt_ref[2 * hp:2 * hp + 2].reshape(2 * HEAD_DIM, W_MAX).astype(BF16)
        vn2 = vn_ref[2 * hp:2 * hp + 2].reshape(2 * HEAD_DIM, LANES).astype(BF16)
        mixed = []
        for half in range(2):
            pb, den, lse = scores(2 * hp + half)
            acc = (lax.dot_general(pb[:, 0:W_MAX], vt2, NT_DIMS, preferred_element_type=F32)
                   + lax.dot_general(pb[:, W_MAX:ncol], vn2, NT_DIMS, preferred_element_type=F32))
            mixed.append(mix_groups(acc / den, lse))
        o_ref[:, hp * LANES:(hp + 1) * LANES] = jnp.where(lane < HEAD_DIM, mixed[0], mixed[1])

    if shift_refs:
        lane_s = lax.broadcasted_iota(jnp.int32, (HEAD_DIM, LANES), 1)
        n_chunks = W_MAX // LANES
        for src_ref, new_ref, dst_ref in ((kt_ref, kn_ref, shift_refs[0]), (vt_ref, vn_ref, shift_refs[1])):
            for h in range(n_heads):
                rolled = [pltpu.roll(src_ref[h, :, c * LANES:(c + 1) * LANES], new0, 1) for c in range(n_chunks)]
                rolled.append(new_ref[h])
                for c in range(n_chunks):
                    dst_ref[h, :, c * LANES:(c + 1) * LANES] = jnp.where(lane_s < new0, rolled[c], rolled[c + 1])


def _attn_sample(q4, kt, vt, knp, vnp, slope4, *, dec_seq, heads_per_step, shift):
    nb = q4.shape[0]
    hps = heads_per_step
    nrow = N_DIL * dec_seq
    per_head = lambda last2: pl.BlockSpec((None, hps) + last2, lambda b, c: (b, c, 0, 0))
    out_specs = [pl.BlockSpec((None, dec_seq, hps * HEAD_DIM), lambda b, c: (b, 0, c))]
    out_shape = [jax.ShapeDtypeStruct((nb, dec_seq, D_MODEL), F32)]
    if shift:
        out_specs += [per_head((HEAD_DIM, W_MAX))] * 2
        out_shape += [jax.ShapeDtypeStruct(kt.shape, kt.dtype)] * 2
    return pl.pallas_call(
        functools.partial(_attn_sample_kernel, dec_seq=dec_seq),
        grid=(nb, N_HEADS // hps),
        in_specs=[per_head((nrow, HEAD_DIM)), per_head((HEAD_DIM, W_MAX)), per_head((HEAD_DIM, W_MAX)),
                  per_head((HEAD_DIM, LANES)), per_head((HEAD_DIM, LANES)),
                  pl.BlockSpec((hps, nrow, LANES), lambda b, c: (c, 0, 0))],
        out_specs=out_specs,
        out_shape=out_shape,
        compiler_params=pltpu.CompilerParams(dimension_semantics=("arbitrary", "arbitrary"),
                                             vmem_limit_bytes=VMEM_LIMIT),
        name="attn_sample_shift" if shift else "attn_sample",
    )(q4, kt, vt, knp, vnp, slope4)


def _attn_out_kernel(x_ref, o0_ref, o1_ref, o2_ref, l0_ref, l1_ref, l2_ref, os_ref, e_ref, wo_ref, p_ref,
                     lng_ref, lnb_ref, wrt_ref, br_ref, r_ref, cls_ref, nat_o, nat_l, *, n_prompt_steps):
    s = pl.program_id(0)

    def finish(o):
        mix = jnp.dot(o.astype(BF16), wo_ref[...], preferred_element_type=F32)
        _tail(x_ref[...], mix, p_ref[...], lng_ref[...], lnb_ref[...], wrt_ref, br_ref, r_ref, cls_ref)

    def natural(ref, scr):
        dil = ref.shape[0]
        if dil == 1:
            return ref[0].astype(F32)
        n_chunk = ref.shape[2] // LANES
        for r in range(dil):
            v = ref[r].astype(F32)
            for c in range(n_chunk):
                scr.at[c][pl.ds(r, TT // dil, stride=dil), :] = v[:, c * LANES:(c + 1) * LANES]
        return jnp.concatenate([scr[c] for c in range(n_chunk)], axis=1)

    @pl.when(s < n_prompt_steps)
    def _():
        lses = [natural(l_ref, nat_l.at[g]) for g, l_ref in enumerate((l0_ref, l1_ref, l2_ref))]
        top = jnp.maximum(jnp.maximum(lses[0], lses[1]), lses[2])
        ws = [jnp.exp(l - top) for l in lses]
        tot = ws[0] + ws[1] + ws[2]
        o = jnp.zeros((TT, D_MODEL), F32)
        for wgt, o_ref in zip(ws, (o0_ref, o1_ref, o2_ref)):
            hi, lo = _split_bf16(wgt / tot)
            wide = (jnp.dot(hi, e_ref[...], preferred_element_type=F32)
                    + jnp.dot(lo, e_ref[...], preferred_element_type=F32))
            o = o + wide * natural(o_ref, nat_o)
        finish(o)

    @pl.when(s >= n_prompt_steps)
    def _():
        finish(os_ref[...])


def _attn_out_layer(x, o3, l3, o_s, expand, wo, p_flat, lng, lnb, wrt, br, *, n_prompt, seq_len):
    ntok = p_flat.shape[0]
    npst = n_prompt // TT
    sps = seq_len // TT
    blk = lambda s: (s, 0)
    sblk = lambda s: (jnp.maximum(s - npst, 0), 0)
    const = lambda s: (0, 0)

    def res_spec(dil, width):
        def im(s):
            sp = jnp.minimum(s, npst - 1)
            return (sp // sps, 0, sp % sps, 0)
        return pl.BlockSpec((None, dil, TT // dil, width), im)

    return pl.pallas_call(
        functools.partial(_attn_out_kernel, n_prompt_steps=npst),
        grid=(ntok // TT,),
        in_specs=[pl.BlockSpec((TT, D_MODEL), blk)]
        + [res_spec(dil, D_MODEL) for _, dil in DIL_PATTERNS]
        + [res_spec(dil, LANES) for _, dil in DIL_PATTERNS]
        + [pl.BlockSpec((TT, D_MODEL), sblk),
           pl.BlockSpec((LANES, D_MODEL), const),
           pl.BlockSpec((D_MODEL, D_MODEL), const),
           pl.BlockSpec((TT, D_PLE), blk),
           pl.BlockSpec((1, D_MODEL), const),
           pl.BlockSpec((1, D_MODEL), const),
           pl.BlockSpec((N_EXPERTS, D_MODEL), const),
           pl.BlockSpec((N_EXPERTS, 1), const)],
        out_specs=[pl.BlockSpec((TT, ROW_W), blk), pl.BlockSpec((1, TT), lambda s: (0, s))],
        out_shape=[jax.ShapeDtypeStruct((ntok, ROW_W), F32), jax.ShapeDtypeStruct((1, ntok), jnp.int32)],
        scratch_shapes=[pltpu.VMEM((N_CHUNK, TT, LANES), F32), pltpu.VMEM((N_DIL, 1, TT, LANES), F32)],
        compiler_params=pltpu.CompilerParams(dimension_semantics=("arbitrary",),
                                             vmem_limit_bytes=VMEM_LIMIT),
        name="attn_out",
    )(x, *o3, *l3, o_s, expand, wo, p_flat, lng, lnb, wrt, br)


def kernel(x_prompt, x_sample, state_pool, cache_k, cache_v, p_prompt, p_sample, w_pool, pool_scale, w_kv,
           w_q, w_o, ln_g, ln_b, w_router, b_router, w_exp_gate, w_exp_up, w_exp_down, w_ple_gate,
           w_ple_proj):
    batch, seq_len, d = x_prompt.shape
    nb, dec_seq, _ = x_sample.shape
    wbuf = cache_k.shape[1]
    n_prompt = batch * seq_len
    n_sample = nb * dec_seq
    ntok = n_prompt + n_sample
    assert d == D_MODEL and wbuf == W_MAX and dec_seq == 8
    assert seq_len % (QB * DIL_PATTERNS[2][1]) == 0 and n_sample % TT == 0 and seq_len % TT == 0

    wrt2 = w_router.T.astype(BF16)
    br = b_router.astype(F32).reshape(N_EXPERTS, 1)
    wg, wu, wd = w_exp_gate.astype(BF16), w_exp_up.astype(BF16), w_exp_down.astype(BF16)
    wpg, wpp = w_ple_gate.astype(BF16), w_ple_proj.astype(BF16)
    wp = w_pool.astype(BF16)
    head_of_col = np.arange(D_MODEL) // HEAD_DIM
    expand = jnp.asarray(np.arange(LANES)[:, None] == head_of_col[None, :], BF16)
    slope4 = jnp.asarray(np.ascontiguousarray(np.broadcast_to(
        np.repeat(SLOPES.T, dec_seq, axis=1)[:, :, None], (N_HEADS, N_DIL * dec_seq, LANES))))

    p_flat = jnp.concatenate([p_prompt.reshape(DEPTH, n_prompt, D_PLE),
                              p_sample.reshape(DEPTH, n_sample, D_PLE)], axis=1)

    def moe_stage(rows, cls, layer):
        slot_tok, elo, ehi, nrows = _route_plan(cls.reshape(ntok))
        return _moe_layer(rows, slot_tok, elo, ehi, nrows, wg, wu, wd, wpg[layer], wpp[layer],
                          ln_g[layer, 1:3], ln_b[layer, 1:3], layer=layer)

    def to_heads_minor(a, width):
        a = a.reshape(nb, dec_seq, N_HEADS, HEAD_DIM).transpose(0, 2, 3, 1)
        return jnp.pad(a, ((0, 0), (0, 0), (0, 0), (width - dec_seq, 0)))

    x = x_prompt.reshape(n_prompt, D_MODEL)
    xs = x_sample
    pool_p, pool_s = [], []
    for i in range(N_A):
        ext_s = jnp.concatenate([jnp.zeros((nb, HALO - POOL_HIST, D_MODEL), F32), state_pool[i], xs], axis=1)
        pool_p.append(jnp.stack([x[(b + 1) * seq_len - POOL_HIST:(b + 1) * seq_len] for b in range(batch)], 0))
        pool_s.append(ext_s[:, HALO + dec_seq - POOL_HIST:])
        rows, cls = _pool_layer(x, ext_s, p_flat[i], wp[i], pool_scale[i].reshape(1, D_MODEL),
                                ln_g[i, 0].reshape(1, D_MODEL), ln_b[i, 0].reshape(1, D_MODEL), wrt2, br,
                                n_prompt=n_prompt, seq_len=seq_len)
        x = moe_stage(rows, cls, i)
        xs = x[n_prompt:ntok].reshape(nb, dec_seq, D_MODEL)

    w_kv_b = w_kv.astype(BF16)
    kt_p, vt_p, *kv_res = _kv_proj_prompt(x, w_kv_b, batch=batch, seq_len=seq_len)
    k_res, v_res = kv_res[:N_DIL], kv_res[N_DIL:]
    kv_s = _proj_sample(x, w_kv_b, n_prompt=n_prompt, n_sample=n_sample, scale=1.0)
    knp = to_heads_minor(kv_s[:, :D_MODEL], LANES)
    vnp = to_heads_minor(kv_s[:, D_MODEL:], LANES)
    kt = cache_k.transpose(0, 2, 3, 1)
    vt = cache_v.transpose(0, 2, 3, 1)

    for jl in range(N_B):
        i = N_A + jl
        w_q_b = w_q[jl].astype(BF16)
        q_res = _q_proj_prompt(x, w_q_b, batch=batch, seq_len=seq_len)
        o3, l3 = [], []
        for g in range(N_DIL):
            o_g, l_g = _attn_prompt(q_res[g], k_res[g], v_res[g], group=g)
            o3.append(o_g)
            l3.append(l_g)
        q_s = _proj_sample(x, w_q_b, n_prompt=n_prompt, n_sample=n_sample, scale=HEAD_DIM ** -0.5)
        q4 = (q_s.reshape(nb, dec_seq, N_DIL, N_HEADS, HEAD_DIM)
              .transpose(0, 3, 2, 1, 4).reshape(nb, N_HEADS, N_DIL * dec_seq, HEAD_DIM))
        if jl == 0:
            o_s, kt_new, vt_new = _attn_sample(q4, kt, vt, knp, vnp, slope4, dec_seq=dec_seq,
                                               heads_per_step=4, shift=True)
        else:
            (o_s,) = _attn_sample(q4, kt, vt, knp, vnp, slope4, dec_seq=dec_seq, heads_per_step=8, shift=False)
        rows, cls = _attn_out_layer(x, o3, l3, o_s.reshape(n_sample, D_MODEL), expand, w_o[jl].astype(BF16),
                                    p_flat[i], ln_g[i, 0].reshape(1, D_MODEL), ln_b[i, 0].reshape(1, D_MODEL),
                                    wrt2, br, n_prompt=n_prompt, seq_len=seq_len)
        x = moe_stage(rows, cls, i)

    kept = kt_p.shape[2]
    k_prompt = kt_p.reshape(batch, N_HEADS, HEAD_DIM, kept).transpose(0, 3, 1, 2)
    v_prompt = vt_p.reshape(batch, N_HEADS, HEAD_DIM, kept).transpose(0, 3, 1, 2)
    return (x[:n_prompt].reshape(batch, seq_len, D_MODEL),
            x[n_prompt:ntok].reshape(nb, dec_seq, D_MODEL),
            jnp.stack(pool_p, 0), jnp.stack(pool_s, 0),
            k_prompt, v_prompt,
            kt_new.transpose(0, 3, 1, 2), vt_new.transpose(0, 3, 1, 2))
```

```python
import functools

import numpy as np
import jax
import jax.numpy as jnp
from jax import lax
from jax.experimental import pallas as pl
from jax.experimental.pallas import tpu as pltpu

D_MODEL = 1024
DEPTH = 4
N_A = DEPTH // 2
N_B = DEPTH - N_A
POOL_WINDOWS = (2, 4, 8, 16)
POOL_GROUP = D_MODEL // len(POOL_WINDOWS)
POOL_HIST = max(POOL_WINDOWS) - 1
HEAD_DIM = 64
N_HEADS = D_MODEL // HEAD_DIM
DIL_PATTERNS = ((128, 1), (512, 4), (2048, 16))
N_DIL = len(DIL_PATTERNS)
W_MAX = 2048
N_EXPERTS = 16
N_EXPERT_GROUPS = 4
EXPERTS_PER_GROUP = 4
D_EXPERT = 512
D_PLE = 256
ALPHA = (2 * DEPTH) ** 0.25
LN_EPS = 1e-5

PAIRS = ((0, 1), (0, 2), (0, 3), (1, 2), (1, 3), (2, 3))
N_CLASSES = N_EXPERT_GROUPS * len(PAIRS)

LANES = 128
HALO = 16
TT = 512
TM = 256
DMA_UNROLL = 8
assert D_MODEL == 8 * LANES
QB = 128
ROW_W = D_MODEL + D_PLE + LANES
VMEM_LIMIT = 48 * 1024 * 1024
NEG = -1e30

F32 = jnp.float32
BF16 = jnp.bfloat16
NT_DIMS = (((1,), (1,)), ((), ()))


def _alibi_slopes():
    n = N_DIL * N_HEADS
    return (2.0 ** (-8.0 * np.arange(1, n + 1) / n)).astype(np.float32).reshape(N_DIL, N_HEADS)


SLOPES = _alibi_slopes()


def _ln(x, g, b):
    mu = jnp.mean(x, axis=-1, keepdims=True)
    xc = x - mu
    var = jnp.mean(xc * xc, axis=-1, keepdims=True)
    return xc * lax.rsqrt(var + LN_EPS) * g + b


def _split_bf16(x):
    hi = x.astype(BF16)
    lo = (x - hi.astype(F32)).astype(BF16)
    return hi, lo


def _route(x1, wrt_ref, br_ref):
    z = lax.dot_general(wrt_ref[...], x1.astype(BF16), NT_DIMS, preferred_element_type=F32)
    s = jax.nn.sigmoid(z)
    sel = s + br_ref[...]
    a = [sel[k:k + 1, :] for k in range(N_EXPERTS)]
    sv = [s[k:k + 1, :] for k in range(N_EXPERTS)]
    one = jnp.ones_like(a[0])
    zero = jnp.zeros_like(a[0])
    gscore, chosen = [], []
    for g in range(N_EXPERT_GROUPS):
        ag = a[4 * g:4 * g + 4]
        cnt = []
        for j in range(4):
            c = zero
            for k in range(4):
                if k == j:
                    continue
                beats = (ag[k] > ag[j]) | (ag[k] == ag[j]) if k < j else (ag[k] > ag[j])
                c = c + jnp.where(beats, one, zero)
            cnt.append(c)
        top = zero
        sec = zero
        for j in range(4):
            top = top + jnp.where(cnt[j] == 0.0, ag[j], zero)
            sec = sec + jnp.where(cnt[j] == 1.0, ag[j], zero)
        gscore.append(top + sec)
        chosen.append([jnp.where(cnt[j] < 2.0, one, zero) for j in range(4)])
    best = gscore[0]
    gi = zero
    for g in range(1, N_EXPERT_GROUPS):
        better = gscore[g] > best
        gi = jnp.where(better, float(g), gi)
        best = jnp.where(better, gscore[g], best)
    m = []
    sg = []
    for j in range(4):
        mj = chosen[0][j]
        sj = sv[j]
        for g in range(1, N_EXPERT_GROUPS):
            isg = gi == float(g)
            mj = jnp.where(isg, chosen[g][j], mj)
            sj = jnp.where(isg, sv[4 * g + j], sj)
        m.append(mj)
        sg.append(sj)
    pidx = zero
    s_lo = zero
    s_hi = zero
    for idx, (p0, p1) in enumerate(PAIRS):
        hit = (m[p0] * m[p1]) > 0.5
        pidx = jnp.where(hit, float(idx), pidx)
        s_lo = jnp.where(hit, sg[p0], s_lo)
        s_hi = jnp.where(hit, sg[p1], s_hi)
    den = s_lo + s_hi
    cls = (gi * float(len(PAIRS)) + pidx).astype(jnp.int32)
    return s_lo / den, s_hi / den, cls


def _tail(x, mix, p, lng, lnb, wrt_ref, br_ref, r_ref, cls_ref):
    x1 = _ln(ALPHA * x + mix, lng, lnb)
    g_lo, g_hi, cls = _route(x1, wrt_ref, br_ref)
    n = x.shape[0]
    row = lax.broadcasted_iota(jnp.int32, (LANES, n), 0)
    meta_t = jnp.where(row == 0, g_lo, jnp.where(row == 1, g_hi, 0.0))
    r_ref[:, 0:D_MODEL] = x1
    r_ref[:, D_MODEL:D_MODEL + D_PLE] = p
    r_ref[:, D_MODEL + D_PLE:ROW_W] = meta_t.T
    cls_ref[...] = cls


def _pool_kernel(x_ref, halo_ref, exts_ref, p_ref, wp_ref, sc_ref, lng_ref, lnb_ref, wrt_ref, br_ref,
                 r_ref, cls_ref, ext_scr, *, n_prompt_steps, steps_per_seq):
    s = pl.program_id(0)

    def finish(x, diff_parts, p):
        ys = [jnp.dot(diff_parts[g].astype(BF16), wp_ref[g], preferred_element_type=F32)
              for g in range(len(POOL_WINDOWS))]
        mix = jnp.concatenate(ys, axis=1) * sc_ref[...]
        _tail(x, mix, p, lng_ref[...], lnb_ref[...], wrt_ref, br_ref, r_ref, cls_ref)

    @pl.when(s < n_prompt_steps)
    def _():
        t = s % steps_per_seq
        keep = jnp.where(t > 0, 1.0, 0.0)
        ext_scr[0:HALO, :] = halo_ref[...] * keep
        ext_scr[HALO:HALO + TT, :] = x_ref[...]
        tpos = (t * TT + lax.broadcasted_iota(jnp.int32, (TT, 1), 0)).astype(F32)
        parts = []
        for g, w in enumerate(POOL_WINDOWS):
            cols = pl.ds(g * POOL_GROUP, POOL_GROUP)
            xg = ext_scr[pl.ds(HALO, TT), cols]
            acc = xg
            for k in range(1, w):
                acc = acc + ext_scr[pl.ds(HALO - k, TT), cols]
            cnt = jnp.minimum(float(w), tpos + 1.0)
            parts.append(acc / cnt - xg)
        finish(x_ref[...], parts, p_ref[...])

    @pl.when(s >= n_prompt_steps)
    def _():
        nb = exts_ref.shape[0]
        hist = exts_ref.shape[1] - HALO
        parts = []
        for g, w in enumerate(POOL_WINDOWS):
            cols = pl.ds(g * POOL_GROUP, POOL_GROUP)
            xg = exts_ref[:, pl.ds(HALO, hist), cols]
            acc = xg
            for k in range(1, w):
                acc = acc + exts_ref[:, pl.ds(HALO - k, hist), cols]
            parts.append((acc / float(w) - xg).reshape(nb * hist, POOL_GROUP))
        x = exts_ref[:, pl.ds(HALO, hist), :].reshape(nb * hist, D_MODEL)
        finish(x, parts, p_ref[...])


def _pool_layer(xp_flat, ext_s, p_flat, wp, sc, lng, lnb, wrt, br, *, n_prompt, seq_len):
    ntok = p_flat.shape[0]
    n_steps = ntok // TT
    npst = n_prompt // TT
    sps = seq_len // TT
    dec_seq = ext_s.shape[1] - HALO
    nb = TT // dec_seq
    kern = functools.partial(_pool_kernel, n_prompt_steps=npst, steps_per_seq=sps)
    const = lambda s: (0, 0)
    return pl.pallas_call(
        kern,
        grid=(n_steps,),
        in_specs=[
            pl.BlockSpec((TT, D_MODEL), lambda s: (jnp.minimum(s, npst - 1), 0)),
            pl.BlockSpec((HALO, D_MODEL),
                         lambda s: (jnp.maximum(jnp.minimum(s, npst - 1) * (TT // HALO) - 1, 0), 0)),
            pl.BlockSpec((nb, HALO + dec_seq, D_MODEL), lambda s: (jnp.maximum(s - npst, 0), 0, 0)),
            pl.BlockSpec((TT, D_PLE), lambda s: (s, 0)),
            pl.BlockSpec((len(POOL_WINDOWS), POOL_GROUP, POOL_GROUP), lambda s: (0, 0, 0)),
            pl.BlockSpec((1, D_MODEL), const),
            pl.BlockSpec((1, D_MODEL), const),
            pl.BlockSpec((1, D_MODEL), const),
            pl.BlockSpec((N_EXPERTS, D_MODEL), const),
            pl.BlockSpec((N_EXPERTS, 1), const),
        ],
        out_specs=[
            pl.BlockSpec((TT, ROW_W), lambda s: (s, 0)),
            pl.BlockSpec((1, TT), lambda s: (0, s)),
        ],
        out_shape=[
            jax.ShapeDtypeStruct((ntok, ROW_W), F32),
            jax.ShapeDtypeStruct((1, ntok), jnp.int32),
        ],
        scratch_shapes=[pltpu.VMEM((HALO + TT, D_MODEL), F32)],
        compiler_params=pltpu.CompilerParams(dimension_semantics=("arbitrary",),
                                             vmem_limit_bytes=VMEM_LIMIT),
        name="pool_mixer",
    )(xp_flat, xp_flat, ext_s, p_flat, wp, sc, lng, lnb, wrt, br)


def _moe_kernel(tok_ref, elo_ref, ehi_ref, nrows_ref, r_hbm, wg_lo, wu_lo, wd_lo, wg_hi, wu_hi, wd_hi,
                wpg_ref, wpp_ref, lng_ref, lnb_ref, x_hbm, rbuf, ybuf, gsem, ssem):
    k = pl.program_id(0)
    n_tiles = pl.num_programs(0)
    slot = k % 2

    def gather_start(tile, s):
        def group(g, c):
            for u in range(DMA_UNROLL):
                j = g * DMA_UNROLL + u
                r = tok_ref[tile * TM + j]
                pltpu.make_async_copy(r_hbm.at[pl.ds(r, 1)], rbuf.at[s, pl.ds(j, 1)],
                                      gsem.at[s]).start(priority=u % 2)
            return c
        lax.fori_loop(0, TM // DMA_UNROLL, group, 0)

    def gather_wait(s):
        pltpu.make_async_copy(r_hbm.at[pl.ds(0, TM)], rbuf.at[s], gsem.at[s]).wait()

    def scatter_row(tile, s, j, priority):
        d = tok_ref[tile * TM + j]
        pltpu.make_async_copy(ybuf.at[s, pl.ds(j, 1)], x_hbm.at[pl.ds(d, 1)], ssem.at[s]).start(priority=priority)

    def scatter_start(tile, s):
        n = nrows_ref[tile]

        def group(g, c):
            for u in range(DMA_UNROLL):
                scatter_row(tile, s, g * DMA_UNROLL + u, u % 2)
            return c

        def single(j, c):
            scatter_row(tile, s, j, 0)
            return c

        lax.fori_loop(0, n // DMA_UNROLL, group, 0)
        lax.fori_loop((n // DMA_UNROLL) * DMA_UNROLL, n, single, 0)

    def scatter_wait(tile, s):
        n = nrows_ref[tile]
        for bit in [1 << b for b in range(TM.bit_length())]:
            if bit >= 8:
                desc = pltpu.make_async_copy(ybuf.at[s, pl.ds(0, bit)], x_hbm.at[pl.ds(0, bit)], ssem.at[s])
            else:
                desc = pltpu.make_async_copy(ybuf.at[s, pl.ds(0, 8), pl.ds(0, bit * LANES)],
                                             x_hbm.at[pl.ds(0, 8), pl.ds(0, bit * LANES)], ssem.at[s])

            @pl.when((n & bit) != 0)
            def _():
                desc.wait()

    next_valid = (k + 1 < n_tiles) & (nrows_ref[jnp.minimum(k + 1, n_tiles - 1)] != 0)

    @pl.when(k == 0)
    def _():
        gather_start(0, 0)

    @pl.when(nrows_ref[k] != 0)
    def _():
        @pl.when(next_valid)
        def _():
            gather_start(k + 1, 1 - slot)

        gather_wait(slot)
        x1 = rbuf[slot, :, 0:D_MODEL]
        p = rbuf[slot, :, D_MODEL:D_MODEL + D_PLE]
        gates = rbuf[slot, :, D_MODEL + D_PLE:ROW_W]
        xb = x1.astype(BF16)

        def expert(wg, wu, wd):
            hg = jnp.dot(xb, wg[...], preferred_element_type=F32)
            hu = jnp.dot(xb, wu[...], preferred_element_type=F32)
            h = (hg * jax.nn.sigmoid(hg)) * hu
            return jnp.dot(h.astype(BF16), wd[...], preferred_element_type=F32)

        moe = gates[:, 0:1] * expert(wg_lo, wu_lo, wd_lo)
        moe = moe + gates[:, 1:2] * expert(wg_hi, wu_hi, wd_hi)
        x2 = _ln(ALPHA * x1 + moe, lng_ref[0:1, :], lnb_ref[0:1, :])
        gate = jax.nn.sigmoid(jnp.dot(x2.astype(BF16), wpg_ref[...], preferred_element_type=F32))
        proj = jnp.dot(p.astype(BF16), wpp_ref[...], preferred_element_type=F32)
        ybuf[slot] = _ln(ALPHA * x2 + gate * proj, lng_ref[1:2, :], lnb_ref[1:2, :])

        @pl.when(k > 0)
        def _():
            scatter_wait(k - 1, 1 - slot)

        scatter_start(k, slot)

        @pl.when(jnp.logical_not(next_valid))
        def _():
            scatter_wait(k, slot)


def _moe_layer(rows, slot_tok, elo, ehi, nrows, wg, wu, wd, wpg, wpp, lng2, lnb2, *, layer):
    ntok = rows.shape[0]
    n_tiles = nrows.shape[0]
    lo = lambda k, tok, elo, ehi, nrows: (layer, elo[k], 0, 0)
    hi = lambda k, tok, elo, ehi, nrows: (layer, ehi[k], 0, 0)
    const2 = lambda k, tok, elo, ehi, nrows: (0, 0)
    gu_spec = lambda im: pl.BlockSpec((None, None, D_MODEL, D_EXPERT), im)
    dn_spec = lambda im: pl.BlockSpec((None, None, D_EXPERT, D_MODEL), im)
    return pl.pallas_call(
        _moe_kernel,
        grid_spec=pltpu.PrefetchScalarGridSpec(
            num_scalar_prefetch=4,
            grid=(n_tiles,),
            in_specs=[
                pl.BlockSpec(memory_space=pl.ANY),
                gu_spec(lo), gu_spec(lo), dn_spec(lo),
                gu_spec(hi), gu_spec(hi), dn_spec(hi),
                pl.BlockSpec((D_MODEL, D_MODEL), const2),
                pl.BlockSpec((D_PLE, D_MODEL), const2),
                pl.BlockSpec((2, D_MODEL), const2),
                pl.BlockSpec((2, D_MODEL), const2),
            ],
            out_specs=pl.BlockSpec(memory_space=pl.ANY),
            scratch_shapes=[pltpu.VMEM((2, TM, ROW_W), F32), pltpu.VMEM((2, TM, D_MODEL), F32),
                            pltpu.SemaphoreType.DMA((2,)), pltpu.SemaphoreType.DMA((2,))],
        ),
        out_shape=jax.ShapeDtypeStruct((ntok, D_MODEL), F32),
        compiler_params=pltpu.CompilerParams(dimension_semantics=("arbitrary",),
                                             vmem_limit_bytes=VMEM_LIMIT),
        name="moe_ple",
    )(slot_tok, elo, ehi, nrows, rows, wg, wu, wd, wg, wu, wd, wpg, wpp, lng2, lnb2)


def _route_plan(cls):
    ntok = cls.shape[0]
    n_tiles = -(-(ntok + N_CLASSES * (TM - 1)) // TM)
    ns = n_tiles * TM
    onehot = (cls[:, None] == jnp.arange(N_CLASSES, dtype=jnp.int32)[None, :]).astype(jnp.int32)
    csum = jnp.cumsum(onehot, axis=0)
    rank = jnp.sum(csum * onehot, axis=1) - 1
    counts = csum[-1]
    padded = ((counts + TM - 1) // TM) * TM
    ends = jnp.cumsum(padded)
    starts = ends - padded
    pos = jnp.sum(onehot * starts[None, :], axis=1) + rank
    tok = jnp.arange(ntok, dtype=jnp.int32)
    src = jnp.zeros((ns,), jnp.int32).at[pos].set(tok)
    tile_start = jnp.arange(n_tiles, dtype=jnp.int32) * TM
    tile_cls = jnp.sum((tile_start[:, None] >= ends[None, :]).astype(jnp.int32), axis=1)
    last_cls = jnp.max(jnp.where(counts > 0, jnp.arange(N_CLASSES, dtype=jnp.int32), 0))
    tile_cls = jnp.minimum(tile_cls, last_cls)
    nrows = jnp.clip((starts + counts)[tile_cls] - tile_start, 0, TM)
    grp = tile_cls // len(PAIRS)
    pidx = tile_cls % len(PAIRS)
    p0 = jnp.asarray([p[0] for p in PAIRS], jnp.int32)[pidx]
    p1 = jnp.asarray([p[1] for p in PAIRS], jnp.int32)[pidx]
    return src, grp * EXPERTS_PER_GROUP + p0, grp * EXPERTS_PER_GROUP + p1, nrows


N_CHUNK = D_MODEL // LANES


def _store_residues(val, scr, out_refs, dils):
    if any(d > 1 for d in dils):
        for c in range(N_CHUNK):
            scr[c] = val[:, c * LANES:(c + 1) * LANES]
    for dil, ref in zip(dils, out_refs):
        if dil == 1:
            ref[0] = val.astype(ref.dtype)
            continue
        for r in range(dil):
            for c in range(N_CHUNK):
                ref[r, :, c * LANES:(c + 1) * LANES] = (
                    scr.at[c][pl.ds(r, TT // dil, stride=dil), :].astype(ref.dtype))


ALL_DILS = tuple(d for _, d in DIL_PATTERNS)


def _kv_prompt_kernel(x_ref, w_ref, kt_ref, vt_ref, k0_ref, k1_ref, k2_ref, v0_ref, v1_ref, v2_ref, scr,
                      *, steps_per_seq, first_kept):
    kv = jnp.dot(x_ref[...].astype(BF16), w_ref[...], preferred_element_type=F32)
    k = kv[:, 0:D_MODEL]
    v = kv[:, D_MODEL:2 * D_MODEL]

    @pl.when(pl.program_id(0) % steps_per_seq >= first_kept)
    def _():
        kt_ref[...] = k.T
        vt_ref[...] = v.T

    _store_residues(k, scr, (k0_ref, k1_ref, k2_ref), ALL_DILS)
    _store_residues(v, scr, (v0_ref, v1_ref, v2_ref), ALL_DILS)


def _q_prompt_kernel(x_ref, w_ref, q0_ref, q1_ref, q2_ref, scr):
    xb = x_ref[...].astype(BF16)
    for g, ref in enumerate((q0_ref, q1_ref, q2_ref)):
        q = jnp.dot(xb, w_ref[:, g * D_MODEL:(g + 1) * D_MODEL], preferred_element_type=F32)
        _store_residues(q * (HEAD_DIM ** -0.5), scr, (ref,), (ALL_DILS[g],))


def _residue_specs(batch, seq_len):
    sps = seq_len // TT
    specs, shapes = [], []
    for _, dil in DIL_PATTERNS:
        specs.append(pl.BlockSpec((None, dil, TT // dil, D_MODEL), lambda s: (s // sps, 0, s % sps, 0)))
        shapes.append(jax.ShapeDtypeStruct((batch, dil, seq_len // dil, D_MODEL), BF16))
    return specs, shapes


def _kv_proj_prompt(x, w, *, batch, seq_len):
    n_prompt = batch * seq_len
    sps = seq_len // TT
    kept = min(W_MAX, seq_len)
    first_kept = (seq_len - kept) // TT
    specs, shapes = _residue_specs(batch, seq_len)
    t_spec = pl.BlockSpec((None, D_MODEL, TT), lambda s: (s // sps, 0, jnp.maximum(s % sps - first_kept, 0)))
    return pl.pallas_call(
        functools.partial(_kv_prompt_kernel, steps_per_seq=sps, first_kept=first_kept),
        grid=(n_prompt // TT,),
        in_specs=[pl.BlockSpec((TT, D_MODEL), lambda s: (s, 0)),
                  pl.BlockSpec((D_MODEL, 2 * D_MODEL), lambda s: (0, 0))],
        out_specs=[t_spec] * 2 + specs + specs,
        out_shape=[jax.ShapeDtypeStruct((batch, D_MODEL, kept), F32)] * 2 + shapes + shapes,
        scratch_shapes=[pltpu.VMEM((N_CHUNK, TT, LANES), F32)],
        compiler_params=pltpu.CompilerParams(dimension_semantics=("arbitrary",),
                                             vmem_limit_bytes=VMEM_LIMIT),
        name="kv_proj_prompt",
    )(x, w)


def _q_proj_prompt(x, w, *, batch, seq_len):
    n_prompt = batch * seq_len
    specs, shapes = _residue_specs(batch, seq_len)
    return pl.pallas_call(
        _q_prompt_kernel,
        grid=(n_prompt // TT,),
        in_specs=[pl.BlockSpec((TT, D_MODEL), lambda s: (s, 0)),
                  pl.BlockSpec((D_MODEL, N_DIL * D_MODEL), lambda s: (0, 0))],
        out_specs=specs,
        out_shape=shapes,
        scratch_shapes=[pltpu.VMEM((N_CHUNK, TT, LANES), F32)],
        compiler_params=pltpu.CompilerParams(dimension_semantics=("arbitrary",),
                                             vmem_limit_bytes=VMEM_LIMIT),
        name="q_proj_prompt",
    )(x, w)


def _proj_sample_kernel(x_ref, w_ref, o_ref, *, scale):
    o_ref[...] = jnp.dot(x_ref[...].astype(BF16), w_ref[...], preferred_element_type=F32) * scale


def _proj_sample(x, w, *, n_prompt, n_sample, scale):
    n_out = w.shape[1]
    first = n_prompt // TT
    return pl.pallas_call(
        functools.partial(_proj_sample_kernel, scale=scale),
        grid=(n_sample // TT, n_out // D_MODEL),
        in_specs=[pl.BlockSpec((TT, D_MODEL), lambda s, c: (first + s, 0)),
                  pl.BlockSpec((D_MODEL, D_MODEL), lambda s, c: (0, c))],
        out_specs=pl.BlockSpec((TT, D_MODEL), lambda s, c: (s, c)),
        out_shape=jax.ShapeDtypeStruct((n_sample, n_out), F32),
        compiler_params=pltpu.CompilerParams(dimension_semantics=("arbitrary", "arbitrary"),
                                             vmem_limit_bytes=VMEM_LIMIT),
        name="proj_sample",
    )(x, w)


def _attn_prompt_kernel(q_ref, kp_ref, kc_ref, vp_ref, vc_ref, o_ref, lse_ref, *, group):
    i = pl.program_id(2)
    dil = DIL_PATTERNS[group][1]
    a = lax.broadcasted_iota(jnp.int32, (2 * QB, 2 * QB), 0) % QB
    j = lax.broadcasted_iota(jnp.int32, (2 * QB, 2 * QB), 1)
    delta = QB + a - j
    ok = (delta >= 0) & (delta <= QB) & ((j >= QB) | (i > 0))
    dist = (delta * dil).astype(F32)
    top_rows = lax.broadcasted_iota(jnp.int32, (2 * QB, 1), 0) < QB
    lane2 = lax.broadcasted_iota(jnp.int32, (2 * QB, LANES), 1)
    row2 = lax.broadcasted_iota(jnp.int32, (2 * QB, LANES), 0)
    own_dims = (lane2 < HEAD_DIM) == (row2 < QB)
    lane = lax.broadcasted_iota(jnp.int32, (QB, LANES), 1)
    lse_blk = jnp.zeros((QB, LANES), F32)
    for hp in range(N_HEADS // 2):
        cols = pl.ds(hp * LANES, LANES)
        q2 = q_ref[:, cols]
        k2 = jnp.concatenate([kp_ref[:, cols], kc_ref[:, cols]], axis=0)
        v2 = jnp.concatenate([vp_ref[:, cols], vc_ref[:, cols]], axis=0)
        qq = jnp.concatenate([q2, q2], axis=0)
        qm = jnp.where(own_dims, qq, jnp.zeros_like(qq))
        slope = jnp.where(top_rows, float(SLOPES[group, 2 * hp]), float(SLOPES[group, 2 * hp + 1]))
        s = lax.dot_general(qm, k2, NT_DIMS, preferred_element_type=F32)
        s = jnp.where(ok, s - slope * dist, NEG)
        mx = jnp.max(s, axis=1, keepdims=True)
        pexp = jnp.exp(s - mx)
        den = jnp.sum(pexp, axis=1, keepdims=True)
        pv = jnp.dot(pexp.astype(BF16), v2, preferred_element_type=F32) / den
        lse = mx + jnp.log(den)
        o_ref[:, cols] = jnp.where(lane < HEAD_DIM, pv[0:QB], pv[QB:2 * QB]).astype(o_ref.dtype)
        lse_blk = jnp.where(lane == 2 * hp, lse[0:QB], jnp.where(lane == 2 * hp + 1, lse[QB:2 * QB], lse_blk))
    lse_ref[...] = lse_blk


def _attn_prompt(qg, kg, vg, *, group):
    batch, dil, rows, _ = qg.shape
    nq = rows // QB
    cur = lambda b, r, i: (b, r, i, 0)
    prev = lambda b, r, i: (b, r, jnp.maximum(i - 1, 0), 0)
    blk = lambda im: pl.BlockSpec((None, None, QB, D_MODEL), im)
    return pl.pallas_call(
        functools.partial(_attn_prompt_kernel, group=group),
        grid=(batch, dil, nq),
        in_specs=[blk(cur), blk(prev), blk(cur), blk(prev), blk(cur)],
        out_specs=[blk(cur), pl.BlockSpec((None, None, QB, LANES), cur)],
        out_shape=[jax.ShapeDtypeStruct((batch, dil, rows, D_MODEL), BF16),
                   jax.ShapeDtypeStruct((batch, dil, rows, LANES), F32)],
        compiler_params=pltpu.CompilerParams(dimension_semantics=("arbitrary",) * 3,
                                             vmem_limit_bytes=VMEM_LIMIT),
        name=f"attn_prompt_g{group}",
    )(qg, kg, kg, vg, vg)


def _attn_sample_kernel(q_ref, kt_ref, vt_ref, kn_ref, vn_ref, sl_ref, o_ref, *shift_refs, dec_seq):
    n_heads = q_ref.shape[0]
    nrow = N_DIL * dec_seq
    ncol = W_MAX + LANES
    new0 = LANES - dec_seq
    row = lax.broadcasted_iota(jnp.int32, (nrow, ncol), 0)
    col = lax.broadcasted_iota(jnp.int32, (nrow, ncol), 1)
    grp = row // dec_seq
    kpos = jnp.where(col < W_MAX, col, col - new0)
    dist = W_MAX + (row - grp * dec_seq) - kpos
    win = jnp.where(grp == 0, DIL_PATTERNS[0][0], jnp.where(grp == 1, DIL_PATTERNS[1][0], DIL_PATTERNS[2][0]))
    dmask = jnp.where(grp == 0, DIL_PATTERNS[0][1] - 1,
                      jnp.where(grp == 1, DIL_PATTERNS[1][1] - 1, DIL_PATTERNS[2][1] - 1))
    ok = ((col < W_MAX) | (col >= W_MAX + new0)) & (dist >= 0) & (dist <= win) & ((dist & dmask) == 0)
    distf = dist.astype(F32)
    lane = lax.broadcasted_iota(jnp.int32, (dec_seq, LANES), 1)

    def scores(h):
        q = q_ref[h].astype(BF16)
        s = jnp.concatenate([jnp.dot(q, kt_ref[h].astype(BF16), preferred_element_type=F32),
                             jnp.dot(q, kn_ref[h].astype(BF16), preferred_element_type=F32)], axis=1)
        s = jnp.where(ok, s - sl_ref[h][:, 0:1] * distf, NEG)
        mx = jnp.max(s, axis=1, keepdims=True)
        pexp = jnp.exp(s - mx)
        den = jnp.sum(pexp, axis=1, keepdims=True)
        return pexp.astype(BF16), den, mx + jnp.log(den)

    def mix_groups(o, lse):
        parts = [o[g * dec_seq:(g + 1) * dec_seq] for g in range(N_DIL)]
        ls = [lse[g * dec_seq:(g + 1) * dec_seq] for g in range(N_DIL)]
        top = jnp.maximum(jnp.maximum(ls[0], ls[1]), ls[2])
        ws = [jnp.exp(l - top) for l in ls]
        return (ws[0] * parts[0] + ws[1] * parts[1] + ws[2] * parts[2]) / (ws[0] + ws[1] + ws[2])

    for hp in range(n_heads // 2):
        vt2 = vt_ref[2 * hp:2 * hp + 2].reshape(2 * HEAD_DIM, W_MAX).astype(BF16)
        vn2 = vn_ref[2 * hp:2 * hp + 2].reshape(2 * HEAD_DIM, LANES).astype(BF16)
        mixed = []
        for half in range(2):
            pb, den, lse = scores(2 * hp + half)
            acc = (lax.dot_general(pb[:, 0:W_MAX], vt2, NT_DIMS, preferred_element_type=F32)
                   + lax.dot_general(pb[:, W_MAX:ncol], vn2, NT_DIMS, preferred_element_type=F32))
            mixed.append(mix_groups(acc / den, lse))
        o_ref[:, hp * LANES:(hp + 1) * LANES] = jnp.where(lane < HEAD_DIM, mixed[0], mixed[1])

    if shift_refs:
        lane_s = lax.broadcasted_iota(jnp.int32, (HEAD_DIM, LANES), 1)
        n_chunks = W_MAX // LANES
        for src_ref, new_ref, dst_ref in ((kt_ref, kn_ref, shift_refs[0]), (vt_ref, vn_ref, shift_refs[1])):
            for h in range(n_heads):
                rolled = [pltpu.roll(src_ref[h, :, c * LANES:(c + 1) * LANES], new0, 1) for c in range(n_chunks)]
                rolled.append(new_ref[h])
                for c in range(n_chunks):
                    dst_ref[h, :, c * LANES:(c + 1) * LANES] = jnp.where(lane_s < new0, rolled[c], rolled[c + 1])


def _attn_sample(q4, kt, vt, knp, vnp, slope4, *, dec_seq, heads_per_step, shift):
    nb = q4.shape[0]
    hps = heads_per_step
    nrow = N_DIL * dec_seq
    per_head = lambda last2: pl.BlockSpec((None, hps) + last2, lambda b, c: (b, c, 0, 0))
    out_specs = [pl.BlockSpec((None, dec_seq, hps * HEAD_DIM), lambda b, c: (b, 0, c))]
    out_shape = [jax.ShapeDtypeStruct((nb, dec_seq, D_MODEL), F32)]
    if shift:
        out_specs += [per_head((HEAD_DIM, W_MAX))] * 2
        out_shape += [jax.ShapeDtypeStruct(kt.shape, kt.dtype)] * 2
    return pl.pallas_call(
        functools.partial(_attn_sample_kernel, dec_seq=dec_seq),
        grid=(nb, N_HEADS // hps),
        in_specs=[per_head((nrow, HEAD_DIM)), per_head((HEAD_DIM, W_MAX)), per_head((HEAD_DIM, W_MAX)),
                  per_head((HEAD_DIM, LANES)), per_head((HEAD_DIM, LANES)),
                  pl.BlockSpec((hps, nrow, LANES), lambda b, c: (c, 0, 0))],
        out_specs=out_specs,
        out_shape=out_shape,
        compiler_params=pltpu.CompilerParams(dimension_semantics=("arbitrary", "arbitrary"),
                                             vmem_limit_bytes=VMEM_LIMIT),
        name="attn_sample_shift" if shift else "attn_sample",
    )(q4, kt, vt, knp, vnp, slope4)


def _attn_out_kernel(x_ref, o0_ref, o1_ref, o2_ref, l0_ref, l1_ref, l2_ref, os_ref, e_ref, wo_ref, p_ref,
                     lng_ref, lnb_ref, wrt_ref, br_ref, r_ref, cls_ref, nat_o, nat_l, *, n_prompt_steps):
    s = pl.program_id(0)

    def finish(o):
        mix = jnp.dot(o.astype(BF16), wo_ref[...], preferred_element_type=F32)
        _tail(x_ref[...], mix, p_ref[...], lng_ref[...], lnb_ref[...], wrt_ref, br_ref, r_ref, cls_ref)

    def natural(ref, scr):
        dil = ref.shape[0]
        if dil == 1:
            return ref[0].astype(F32)
        n_chunk = ref.shape[2] // LANES
        for r in range(dil):
            v = ref[r].astype(F32)
            for c in range(n_chunk):
                scr.at[c][pl.ds(r, TT // dil, stride=dil), :] = v[:, c * LANES:(c + 1) * LANES]
        return jnp.concatenate([scr[c] for c in range(n_chunk)], axis=1)

    @pl.when(s < n_prompt_steps)
    def _():
        lses = [natural(l_ref, nat_l.at[g]) for g, l_ref in enumerate((l0_ref, l1_ref, l2_ref))]
        top = jnp.maximum(jnp.maximum(lses[0], lses[1]), lses[2])
        ws = [jnp.exp(l - top) for l in lses]
        tot = ws[0] + ws[1] + ws[2]
        o = jnp.zeros((TT, D_MODEL), F32)
        for wgt, o_ref in zip(ws, (o0_ref, o1_ref, o2_ref)):
            hi, lo = _split_bf16(wgt / tot)
            wide = (jnp.dot(hi, e_ref[...], preferred_element_type=F32)
                    + jnp.dot(lo, e_ref[...], preferred_element_type=F32))
            o = o + wide * natural(o_ref, nat_o)
        finish(o)

    @pl.when(s >= n_prompt_steps)
    def _():
        finish(os_ref[...])


def _attn_out_layer(x, o3, l3, o_s, expand, wo, p_flat, lng, lnb, wrt, br, *, n_prompt, seq_len):
    ntok = p_flat.shape[0]
    npst = n_prompt // TT
    sps = seq_len // TT
    blk = lambda s: (s, 0)
    sblk = lambda s: (jnp.maximum(s - npst, 0), 0)
    const = lambda s: (0, 0)

    def res_spec(dil, width):
        def im(s):
            sp = jnp.minimum(s, npst - 1)
            return (sp // sps, 0, sp % sps, 0)
        return pl.BlockSpec((None, dil, TT // dil, width), im)

    return pl.pallas_call(
        functools.partial(_attn_out_kernel, n_prompt_steps=npst),
        grid=(ntok // TT,),
        in_specs=[pl.BlockSpec((TT, D_MODEL), blk)]
        + [res_spec(dil, D_MODEL) for _, dil in DIL_PATTERNS]
        + [res_spec(dil, LANES) for _, dil in DIL_PATTERNS]
        + [pl.BlockSpec((TT, D_MODEL), sblk),
           pl.BlockSpec((LANES, D_MODEL), const),
           pl.BlockSpec((D_MODEL, D_MODEL), const),
           pl.BlockSpec((TT, D_PLE), blk),
           pl.BlockSpec((1, D_MODEL), const),
           pl.BlockSpec((1, D_MODEL), const),
           pl.BlockSpec((N_EXPERTS, D_MODEL), const),
           pl.BlockSpec((N_EXPERTS, 1), const)],
        out_specs=[pl.BlockSpec((TT, ROW_W), blk), pl.BlockSpec((1, TT), lambda s: (0, s))],
        out_shape=[jax.ShapeDtypeStruct((ntok, ROW_W), F32), jax.ShapeDtypeStruct((1, ntok), jnp.int32)],
        scratch_shapes=[pltpu.VMEM((N_CHUNK, TT, LANES), F32), pltpu.VMEM((N_DIL, 1, TT, LANES), F32)],
        compiler_params=pltpu.CompilerParams(dimension_semantics=("arbitrary",),
                                             vmem_limit_bytes=VMEM_LIMIT),
        name="attn_out",
    )(x, *o3, *l3, o_s, expand, wo, p_flat, lng, lnb, wrt, br)


def kernel(x_prompt, x_sample, state_pool, cache_k, cache_v, p_prompt, p_sample, w_pool, pool_scale, w_kv,
           w_q, w_o, ln_g, ln_b, w_router, b_router, w_exp_gate, w_exp_up, w_exp_down, w_ple_gate,
           w_ple_proj):
    batch, seq_len, d = x_prompt.shape
    nb, dec_seq, _ = x_sample.shape
    wbuf = cache_k.shape[1]
    n_prompt = batch * seq_len
    n_sample = nb * dec_seq
    ntok = n_prompt + n_sample
    assert d == D_MODEL and wbuf == W_MAX and dec_seq == 8
    assert seq_len % (QB * DIL_PATTERNS[2][1]) == 0 and n_sample % TT == 0 and seq_len % TT == 0

    wrt2 = w_router.T.astype(BF16)
    br = b_router.astype(F32).reshape(N_EXPERTS, 1)
    wg, wu, wd = w_exp_gate.astype(BF16), w_exp_up.astype(BF16), w_exp_down.astype(BF16)
    wpg, wpp = w_ple_gate.astype(BF16), w_ple_proj.astype(BF16)
    wp = w_pool.astype(BF16)
    head_of_col = np.arange(D_MODEL) // HEAD_DIM
    expand = jnp.asarray(np.arange(LANES)[:, None] == head_of_col[None, :], BF16)
    slope4 = jnp.asarray(np.ascontiguousarray(np.broadcast_to(
        np.repeat(SLOPES.T, dec_seq, axis=1)[:, :, None], (N_HEADS, N_DIL * dec_seq, LANES))))

    p_flat = jnp.concatenate([p_prompt.reshape(DEPTH, n_prompt, D_PLE),
                              p_sample.reshape(DEPTH, n_sample, D_PLE)], axis=1)

    def moe_stage(rows, cls, layer):
        slot_tok, elo, ehi, nrows = _route_plan(cls.reshape(ntok))
        return _moe_layer(rows, slot_tok, elo, ehi, nrows, wg, wu, wd, wpg[layer], wpp[layer],
                          ln_g[layer, 1:3], ln_b[layer, 1:3], layer=layer)

    def to_heads_minor(a, width):
        a = a.reshape(nb, dec_seq, N_HEADS, HEAD_DIM).transpose(0, 2, 3, 1)
        return jnp.pad(a, ((0, 0), (0, 0), (0, 0), (width - dec_seq, 0)))

    x = x_prompt.reshape(n_prompt, D_MODEL)
    xs = x_sample
    pool_p, pool_s = [], []
    for i in range(N_A):
        ext_s = jnp.concatenate([jnp.zeros((nb, HALO - POOL_HIST, D_MODEL), F32), state_pool[i], xs], axis=1)
        pool_p.append(jnp.stack([x[(b + 1) * seq_len - POOL_HIST:(b + 1) * seq_len] for b in range(batch)], 0))
        pool_s.append(ext_s[:, HALO + dec_seq - POOL_HIST:])
        rows, cls = _pool_layer(x, ext_s, p_flat[i], wp[i], pool_scale[i].reshape(1, D_MODEL),
                                ln_g[i, 0].reshape(1, D_MODEL), ln_b[i, 0].reshape(1, D_MODEL), wrt2, br,
                                n_prompt=n_prompt, seq_len=seq_len)
        x = moe_stage(rows, cls, i)
        xs = x[n_prompt:ntok].reshape(nb, dec_seq, D_MODEL)

    w_kv_b = w_kv.astype(BF16)
    kt_p, vt_p, *kv_res = _kv_proj_prompt(x, w_kv_b, batch=batch, seq_len=seq_len)
    k_res, v_res = kv_res[:N_DIL], kv_res[N_DIL:]
    kv_s = _proj_sample(x, w_kv_b, n_prompt=n_prompt, n_sample=n_sample, scale=1.0)
    knp = to_heads_minor(kv_s[:, :D_MODEL], LANES)
    vnp = to_heads_minor(kv_s[:, D_MODEL:], LANES)
    kt = cache_k.transpose(0, 2, 3, 1)
    vt = cache_v.transpose(0, 2, 3, 1)

    for jl in range(N_B):
        i = N_A + jl
        w_q_b = w_q[jl].astype(BF16)
        q_res = _q_proj_prompt(x, w_q_b, batch=batch, seq_len=seq_len)
        o3, l3 = [], []
        for g in range(N_DIL):
            o_g, l_g = _attn_prompt(q_res[g], k_res[g], v_res[g], group=g)
            o3.append(o_g)
            l3.append(l_g)
        q_s = _proj_sample(x, w_q_b, n_prompt=n_prompt, n_sample=n_sample, scale=HEAD_DIM ** -0.5)
        q4 = (q_s.reshape(nb, dec_seq, N_DIL, N_HEADS, HEAD_DIM)
              .transpose(0, 3, 2, 1, 4).reshape(nb, N_HEADS, N_DIL * dec_seq, HEAD_DIM))
        if jl == 0:
            o_s, kt_new, vt_new = _attn_sample(q4, kt, vt, knp, vnp, slope4, dec_seq=dec_seq,
                                               heads_per_step=8, shift=True)
        else:
            (o_s,) = _attn_sample(q4, kt, vt, knp, vnp, slope4, dec_seq=dec_seq, heads_per_step=8, shift=False)
        rows, cls = _attn_out_layer(x, o3, l3, o_s.reshape(n_sample, D_MODEL), expand, w_o[jl].astype(BF16),
                                    p_flat[i], ln_g[i, 0].reshape(1, D_MODEL), ln_b[i, 0].reshape(1, D_MODEL),
                                    wrt2, br, n_prompt=n_prompt, seq_len=seq_len)
        x = moe_stage(rows, cls, i)

    kept = kt_p.shape[2]
    k_prompt = kt_p.reshape(batch, N_HEADS, HEAD_DIM, kept).transpose(0, 3, 1, 2)
    v_prompt = vt_p.reshape(batch, N_HEADS, HEAD_DIM, kept).transpose(0, 3, 1, 2)
    return (x[:n_prompt].reshape(batch, seq_len, D_MODEL),
            x[n_prompt:ntok].reshape(nb, dec_seq, D_MODEL),
            jnp.stack(pool_p, 0), jnp.stack(pool_s, 0),
            k_prompt, v_prompt,
            kt_new.transpose(0, 3, 1, 2), vt_new.transpose(0, 3, 1, 2))
```
